```python
import math
import functools
import jax
import jax.numpy as jnp
from jax import lax

D_MODEL = 1024
BATCH = 2
SEQ = 8192
DEPTH = 2
DEC_BATCH = 128
DEC_SEQ = 8
PAST_LEN = 16384
PAGE_SIZE = 128

POOL_WIDTH = D_MODEL // 4
POOL_WINDOWS = (2, 4, 8, 16)
POOL_GROUPS = len(POOL_WINDOWS)
POOL_GDIM = POOL_WIDTH // POOL_GROUPS
POOL_BUF = max(POOL_WINDOWS) - 1

RWKV_HEAD = 64
RWKV_WIDTH = D_MODEL // 4
RWKV_HEADS = RWKV_WIDTH // RWKV_HEAD
RWKV_DECAY_LORA = 16
RWKV_AAA_LORA = 16
RWKV_MV_LORA = 8
RWKV_GATE_LORA = 32
RWKV_GN_EPS = 64e-5

MLA_HEADS = 8
MLA_V_DIM = 64
MLA_WIDTH = MLA_HEADS * MLA_V_DIM
MLA_NOPE = 64
MLA_ROPE = 32
MLA_Q_RANK = 384
MLA_KV_RANK = 256
MLA_SCALE = (MLA_NOPE + MLA_ROPE) ** -0.5
ROPE_BASE = 10000.0
QUERY_BLOCK = 128

MIX_WIDTH = POOL_WIDTH + RWKV_WIDTH + MLA_WIDTH
RWKV_COLS = 3 * RWKV_WIDTH + RWKV_DECAY_LORA + RWKV_AAA_LORA + RWKV_GATE_LORA
MLA_COLS = MLA_Q_RANK + MLA_KV_RANK + MLA_ROPE
P_IN = POOL_WIDTH + RWKV_COLS + MLA_COLS

D_FF = 2816
N_EXPERTS = 8
TOP_K = 2
D_FF_EXPERT = 3584
N_DENSE = (DEPTH + 1) // 2
N_MOE = DEPTH // 2

DEEPNORM_ALPHA = (2 * DEPTH) ** 0.25
DEEPNORM_BETA = (8 * DEPTH) ** -0.25
LN_EPS = 1e-5
RMS_EPS = 1e-6

kernel_name = 'hybrid_pool_rwkv7_mla_decoder_step'


def layer_norm(x, g, b):
    xf = x.astype(jnp.float32)
    mu = jnp.mean(xf, axis=-1, keepdims=True)
    var = jnp.mean(jnp.square(xf - mu), axis=-1, keepdims=True)
    return ((xf - mu) * lax.rsqrt(var + LN_EPS) * g + b).astype(x.dtype)


def rms_norm(x, g):
    xf = x.astype(jnp.float32)
    return (xf * lax.rsqrt(jnp.mean(xf * xf, axis=-1, keepdims=True) + RMS_EPS) * g).astype(x.dtype)


def rope(x, pos):
    half = x.shape[-1] // 2
    freqs = ROPE_BASE ** (-jnp.arange(half, dtype=jnp.float32) / half)
    ang = pos.astype(jnp.float32)[:, None] * freqs[None, :]
    cos = jnp.cos(ang)[None, :, None, :]
    sin = jnp.sin(ang)[None, :, None, :]
    xf = x.astype(jnp.float32)
    x1, x2 = xf[..., :half], xf[..., half:]
    return jnp.concatenate([x1 * cos - x2 * sin, x1 * sin + x2 * cos], axis=-1).astype(x.dtype)


def pool_mix(p, prefix, pos, w_grp, scale):
    B, S, _ = p.shape
    ext = jnp.concatenate([prefix.astype(p.dtype), p], axis=1)
    extf = ext.astype(jnp.float32)
    cs = jnp.concatenate([jnp.zeros_like(extf[:, :1]), jnp.cumsum(extf, axis=1)], axis=1)
    means = []
    for g, w in enumerate(POOL_WINDOWS):
        csg = cs[..., g * POOL_GDIM:(g + 1) * POOL_GDIM]
        win = csg[:, POOL_BUF + 1:] - csg[:, POOL_BUF + 1 - w:POOL_BUF + 1 - w + S]
        count = jnp.minimum(pos + 1, w).astype(jnp.float32)
        means.append(win / count[None, :, None])
    d = (jnp.concatenate(means, axis=-1) - p.astype(jnp.float32)).reshape(B, S, POOL_GROUPS, POOL_GDIM)
    y = jnp.einsum('bsgc,gcd->bsgd', d, w_grp).reshape(B, S, POOL_WIDTH) * scale
    return y.astype(p.dtype), ext[:, -POOL_BUF:]


def wkv_scan(r, decay, k, v, kk, a, state0):
    def step(st, inp):
        r_t, w_t, k_t, v_t, kk_t, a_t = inp
        sa = jnp.einsum('bhij,bhj->bhi', st, -kk_t)
        st = (st * w_t[:, :, None, :] + sa[..., None] * (kk_t * a_t)[:, :, None, :]
              + v_t[..., None] * k_t[:, :, None, :])
        return st, jnp.einsum('bhij,bhj->bhi', st, r_t)
    xs = tuple(jnp.swapaxes(t, 0, 1) for t in (r, decay, k, v, kk, a))
    st_fin, ys = lax.scan(step, state0, xs)
    return jnp.swapaxes(ys, 0, 1), st_fin


def rwkv_mix(pb, shift_prev, wkv0, v_first, lp, vres):
    f32 = jnp.float32
    B, S, _ = pb.shape
    pbf = pb.astype(f32)
    prev = jnp.concatenate([shift_prev.astype(f32)[:, None], pbf[:, :-1]], axis=1)
    xs = pbf + (prev - pbf) * lp['rwkv_mu']
    W, DL, AL = RWKV_WIDTH, RWKV_DECAY_LORA, RWKV_AAA_LORA
    r, w_in, k, v, a_in, g_in = jnp.split(xs, [W, W + DL, 2 * W + DL, 3 * W + DL, 3 * W + DL + AL], axis=-1)
    w_log = -jax.nn.softplus(-(lp['rwkv_w0'] + jnp.tanh(w_in) @ lp['rwkv_w2'])) - 0.5
    decay = jnp.exp(-jnp.exp(w_log))
    if vres is None:
        v_first = v
    else:
        v0, v1, v2 = vres
        v = v + (v_first - v) * jax.nn.sigmoid(v0 + (v @ v1) @ v2)
    a = jax.nn.sigmoid(lp['rwkv_a0'] + a_in @ lp['rwkv_a2'])
    g = jax.nn.sigmoid(g_in) @ lp['rwkv_g2']
    kk = k * lp['rwkv_kk']
    k = k * (1.0 + (a - 1.0) * lp['rwkv_ka'])
    heads = lambda t: t.reshape(B, S, RWKV_HEADS, RWKV_HEAD)
    r, decay, k, v, kk, a = (heads(t) for t in (r, decay, k, v, kk, a))
    kk = kk / jnp.maximum(jnp.sqrt(jnp.sum(kk * kk, axis=-1, keepdims=True)), 1e-12)
    y, wkv_fin = wkv_scan(r, decay, k, v, kk, a, wkv0.astype(f32))
    mu = jnp.mean(y, axis=-1, keepdims=True)
    var = jnp.mean(jnp.square(y - mu), axis=-1, keepdims=True)
    y = ((y - mu) * lax.rsqrt(var + RWKV_GN_EPS)).reshape(B, S, RWKV_WIDTH) * lp['rwkv_lnx_g'] + lp['rwkv_lnx_b']
    bonus = jnp.sum(r * k * lp['rwkv_rk'], axis=-1, keepdims=True) * v
    y = (y + bonus.reshape(B, S, RWKV_WIDTH)) * g
    return y.astype(pb.dtype), pb[:, -1], wkv_fin.astype(wkv0.dtype), v_first


def mla_project(pc, pos, lp):
    q_c = pc[..., :MLA_Q_RANK]
    kv_c = pc[..., MLA_Q_RANK:MLA_Q_RANK + MLA_KV_RANK]
    k_r = pc[..., MLA_Q_RANK + MLA_KV_RANK:]
    q = jnp.einsum('bsr,rhd->bshd', rms_norm(q_c, lp['mla_qnorm_g']), lp['mla_wuq'])
    q_rope = rope(q[..., MLA_NOPE:], pos)
    q_lat = jnp.einsum('bshn,chn->bshc', q[..., :MLA_NOPE], lp['mla_wuk'])
    ckv = rms_norm(kv_c, lp['mla_kvnorm_g'])
    krope = rope(k_r[:, :, None, :], pos)[:, :, 0, :]
    return q_lat, q_rope, ckv, krope


def mla_attend_prompt(q_lat, q_rope, ckv, krope):
    B, S, H, C = q_lat.shape
    qb = math.gcd(S, QUERY_BLOCK)
    nb = S // qb
    kpos = jnp.arange(S)

    def block(args):
        ql, qr, qpos = args
        s = (jnp.einsum('bqhc,bkc->bhqk', ql, ckv, preferred_element_type=jnp.float32)
             + jnp.einsum('bqhr,bkr->bhqk', qr, krope, preferred_element_type=jnp.float32)) * MLA_SCALE
        s = jnp.where(kpos[None, None, None, :] <= qpos[None, None, :, None], s, -jnp.inf)
        p = jax.nn.softmax(s, axis=-1)
        return jnp.einsum('bhqk,bkc->bqhc', p, ckv)

    ql_b = jnp.swapaxes(q_lat.reshape(B, nb, qb, H, C), 0, 1)
    qr_b = jnp.swapaxes(q_rope.reshape(B, nb, qb, H, MLA_ROPE), 0, 1)
    qpos_b = jnp.arange(S).reshape(nb, qb)
    out = lax.map(block, (ql_b, qr_b, qpos_b))
    return jnp.swapaxes(out, 0, 1).reshape(B, S, H, C)


def mla_attend_sample(ckv_pool, krope_pool, layer, page_table, q_lat, q_rope, ckv_new, krope_new):
    Bd, S, H, C = q_lat.shape
    ckv_past = ckv_pool[layer, page_table].reshape(Bd, -1, C)
    kr_past = krope_pool[layer, page_table].reshape(Bd, -1, MLA_ROPE)
    s_past = (jnp.einsum('bqhc,bkc->bhqk', q_lat, ckv_past, preferred_element_type=jnp.float32)
              + jnp.einsum('bqhr,bkr->bhqk', q_rope, kr_past, preferred_element_type=jnp.float32)) * MLA_SCALE
    s_new = (jnp.einsum('bqhc,bkc->bhqk', q_lat, ckv_new, preferred_element_type=jnp.float32)
             + jnp.einsum('bqhr,bkr->bhqk', q_rope, krope_new, preferred_element_type=jnp.float32)) * MLA_SCALE
    causal = jnp.arange(S)[None, :] <= jnp.arange(S)[:, None]
    s_new = jnp.where(causal[None, None], s_new, -jnp.inf)
    m = jnp.maximum(jnp.max(s_past, axis=-1, keepdims=True), jnp.max(s_new, axis=-1, keepdims=True))
    e_past = jnp.exp(s_past - m)
    e_new = jnp.exp(s_new - m)
    den = jnp.sum(e_past, axis=-1, keepdims=True) + jnp.sum(e_new, axis=-1, keepdims=True)
    o = jnp.einsum('bhqk,bkc->bqhc', e_past, ckv_past) + jnp.einsum('bhqk,bkc->bqhc', e_new, ckv_new)
    return o / jnp.swapaxes(den, 1, 2)


def token_mixer(h, pos, pool_buf, shift_buf, wkv_state, v_first, attend, lp):
    B, S, _ = h.shape
    proj = h @ lp['w_in']
    pa = proj[..., :POOL_WIDTH]
    pb = proj[..., POOL_WIDTH:POOL_WIDTH + RWKV_COLS]
    pc = proj[..., POOL_WIDTH + RWKV_COLS:]
    ya, new_pool = pool_mix(pa, pool_buf, pos, lp['pool_w'], lp['pool_scale'])
    yb, new_shift, new_wkv, v_first = rwkv_mix(pb, shift_buf, wkv_state, v_first, lp, lp['vres'])
    q_lat, q_rope, ckv, krope = mla_project(pc, pos, lp)
    o_lat = attend(q_lat, q_rope, ckv, krope)
    yc = jnp.einsum('bshc,chn->bshn', o_lat, lp['mla_wuv']).reshape(B, S, MLA_WIDTH)
    y = jnp.concatenate([ya, yb, yc.astype(h.dtype)], axis=-1) @ lp['w_out']
    return y, (ckv, krope, new_pool, new_shift, new_wkv), v_first


def swiglu(x, wg, wu, wd):
    return (jax.nn.silu(x @ wg) * (x @ wu)) @ wd


def moe_swiglu(x, router, wg, wu, wd):
    logits = jnp.einsum('bsd,de->bse', x, router, preferred_element_type=jnp.float32)
    top_val, top_idx = lax.top_k(logits, TOP_K)
    top_gate = jax.nn.softmax(top_val, axis=-1)
    gate = jnp.sum(jax.nn.one_hot(top_idx, N_EXPERTS, dtype=jnp.float32) * top_gate[..., None], axis=-2)
    y = jnp.zeros(x.shape, jnp.float32)
    for e in range(N_EXPERTS):
        y = y + gate[..., e:e + 1] * swiglu(x, wg[e], wu[e], wd[e])
    return y.astype(x.dtype)


def run_trunk(x, pos, pool_bufs, shift_bufs, wkv_states, attends, layers):
    v_first = None
    new = []
    for l in range(DEPTH):
        lp = layers[l]
        h, st, v_first = token_mixer(x, pos, pool_bufs[l], shift_bufs[l], wkv_states[l], v_first, attends[l], lp)
        x = layer_norm(DEEPNORM_ALPHA * x + h, lp['ln1_g'], lp['ln1_b'])
        if l % 2 == 0:
            f = swiglu(x, *lp['ffn'])
        else:
            f = moe_swiglu(x, *lp['moe'])
        x = layer_norm(DEEPNORM_ALPHA * x + f, lp['ln2_g'], lp['ln2_b'])
        new.append(st)
    ckv, krope, pool, shift, wkv = (jnp.stack([s[i] for s in new]) for i in range(5))
    return x, ckv, krope, pool, shift, wkv


def setup_inputs(seed: int = 0) -> dict:
    key = jax.random.key(seed)
    ks = iter(jax.random.split(key, 64))
    f32 = jnp.float32

    def nrm(shape, scale):
        return jax.random.normal(next(ks), shape, f32) * scale

    def gain(shape):
        return 1.0 + nrm(shape, 0.02)

    n_pages = PAST_LEN // PAGE_SIZE
    n_phys = (5 * DEC_BATCH * n_pages) // 4
    perm = jax.random.permutation(next(ks), n_phys)
    page_table = perm[:DEC_BATCH * n_pages].reshape(DEC_BATCH, n_pages).astype(jnp.int32)
    L = DEPTH
    return {
        'x_prompt': nrm((BATCH, SEQ, D_MODEL), 1.0),
        'x_sample': nrm((DEC_BATCH, DEC_SEQ, D_MODEL), 1.0),
        'cache_ckv': nrm((L, n_phys, PAGE_SIZE, MLA_KV_RANK), 1.0),
        'cache_krope': nrm((L, n_phys, PAGE_SIZE, MLA_ROPE), 1.0),
        'state_pool': nrm((L, DEC_BATCH, POOL_BUF, POOL_WIDTH), 1.0),
        'state_shift': nrm((L, DEC_BATCH, RWKV_COLS), 1.0),
        'state_wkv': nrm((L, DEC_BATCH, RWKV_HEADS, RWKV_HEAD, RWKV_HEAD), 0.5),
        'page_table': page_table,
        'ln1_g': gain((L, D_MODEL)),
        'ln1_b': nrm((L, D_MODEL), 0.02),
        'ln2_g': gain((L, D_MODEL)),
        'ln2_b': nrm((L, D_MODEL), 0.02),
        'w_in': nrm((L, D_MODEL, P_IN), D_MODEL ** -0.5),
        'pool_w': nrm((L, POOL_GROUPS, POOL_GDIM, POOL_GDIM), POOL_GDIM ** -0.5),
        'pool_scale': gain((L, POOL_WIDTH)),
        'rwkv_mu': jax.random.uniform(next(ks), (L, RWKV_COLS), f32),
        'rwkv_w0': jax.random.uniform(next(ks), (L, RWKV_WIDTH), f32, -6.5, -1.5),
        'rwkv_w2': nrm((L, RWKV_DECAY_LORA, RWKV_WIDTH), 0.1 * RWKV_DECAY_LORA ** -0.5),
        'rwkv_a0': nrm((L, RWKV_WIDTH), 0.1),
        'rwkv_a2': nrm((L, RWKV_AAA_LORA, RWKV_WIDTH), 0.1 * RWKV_AAA_LORA ** -0.5),
        'rwkv_g2': nrm((L, RWKV_GATE_LORA, RWKV_WIDTH), RWKV_GATE_LORA ** -0.5),
        'rwkv_kk': 0.85 + nrm((L, RWKV_WIDTH), 0.02),
        'rwkv_ka': gain((L, RWKV_WIDTH)),
        'rwkv_rk': nrm((L, RWKV_HEADS, RWKV_HEAD), 0.1),
        'rwkv_lnx_g': gain((L, RWKV_WIDTH)),
        'rwkv_lnx_b': nrm((L, RWKV_WIDTH), 0.02),
        'vres_v0': gain((L - 1, RWKV_WIDTH)),
        'vres_w1': nrm((L - 1, RWKV_WIDTH, RWKV_MV_LORA), RWKV_WIDTH ** -0.5),
        'vres_w2': nrm((L - 1, RWKV_MV_LORA, RWKV_WIDTH), RWKV_MV_LORA ** -0.5),
        'mla_qnorm_g': gain((L, MLA_Q_RANK)),
        'mla_wuq': nrm((L, MLA_Q_RANK, MLA_HEADS, MLA_NOPE + MLA_ROPE), MLA_Q_RANK ** -0.5),
        'mla_kvnorm_g': gain((L, MLA_KV_RANK)),
        'mla_wuk': nrm((L, MLA_KV_RANK, MLA_HEADS, MLA_NOPE), MLA_KV_RANK ** -0.5),
        'mla_wuv': nrm((L, MLA_KV_RANK, MLA_HEADS, MLA_V_DIM), MLA_KV_RANK ** -0.5),
        'w_out': nrm((L, MIX_WIDTH, D_MODEL), DEEPNORM_BETA * MIX_WIDTH ** -0.5),
        'ffn_w_gate': nrm((N_DENSE, D_MODEL, D_FF), D_MODEL ** -0.5),
        'ffn_w_up': nrm((N_DENSE, D_MODEL, D_FF), D_MODEL ** -0.5),
        'ffn_w_down': nrm((N_DENSE, D_FF, D_MODEL), DEEPNORM_BETA * D_FF ** -0.5),
        'moe_router': nrm((N_MOE, D_MODEL, N_EXPERTS), D_MODEL ** -0.5),
        'moe_w_gate': nrm((N_MOE, N_EXPERTS, D_MODEL, D_FF_EXPERT), D_MODEL ** -0.5),
        'moe_w_up': nrm((N_MOE, N_EXPERTS, D_MODEL, D_FF_EXPERT), D_MODEL ** -0.5),
        'moe_w_down': nrm((N_MOE, N_EXPERTS, D_FF_EXPERT, D_MODEL), DEEPNORM_BETA * D_FF_EXPERT ** -0.5),
    }


def reference(x_prompt, x_sample, cache_ckv, cache_krope, state_pool, state_shift, state_wkv, page_table,
              ln1_g, ln1_b, ln2_g, ln2_b, w_in, pool_w, pool_scale,
              rwkv_mu, rwkv_w0, rwkv_w2, rwkv_a0, rwkv_a2, rwkv_g2, rwkv_kk, rwkv_ka, rwkv_rk,
              rwkv_lnx_g, rwkv_lnx_b, vres_v0, vres_w1, vres_w2,
              mla_qnorm_g, mla_wuq, mla_kvnorm_g, mla_wuk, mla_wuv, w_out,
              ffn_w_gate, ffn_w_up, ffn_w_down, moe_router, moe_w_gate, moe_w_up, moe_w_down):
    layers = []
    for l in range(DEPTH):
        lp = dict(ln1_g=ln1_g[l], ln1_b=ln1_b[l], ln2_g=ln2_g[l], ln2_b=ln2_b[l], w_in=w_in[l],
                  pool_w=pool_w[l], pool_scale=pool_scale[l], rwkv_mu=rwkv_mu[l], rwkv_w0=rwkv_w0[l],
                  rwkv_w2=rwkv_w2[l], rwkv_a0=rwkv_a0[l], rwkv_a2=rwkv_a2[l], rwkv_g2=rwkv_g2[l],
                  rwkv_kk=rwkv_kk[l], rwkv_ka=rwkv_ka[l], rwkv_rk=rwkv_rk[l],
                  rwkv_lnx_g=rwkv_lnx_g[l], rwkv_lnx_b=rwkv_lnx_b[l],
                  mla_qnorm_g=mla_qnorm_g[l], mla_wuq=mla_wuq[l], mla_kvnorm_g=mla_kvnorm_g[l],
                  mla_wuk=mla_wuk[l], mla_wuv=mla_wuv[l], w_out=w_out[l])
        lp['vres'] = None if l == 0 else (vres_v0[l - 1], vres_w1[l - 1], vres_w2[l - 1])
        if l % 2 == 0:
            lp['ffn'] = (ffn_w_gate[l // 2], ffn_w_up[l // 2], ffn_w_down[l // 2])
        else:
            lp['moe'] = (moe_router[l // 2], moe_w_gate[l // 2], moe_w_up[l // 2], moe_w_down[l // 2])
        layers.append(lp)

    b, dt = x_prompt.shape[0], x_prompt.dtype
    pos_p = jnp.arange(x_prompt.shape[1], dtype=jnp.int32)
    zero_pool = jnp.zeros((DEPTH, b, POOL_BUF, POOL_WIDTH), dt)
    zero_shift = jnp.zeros((DEPTH, b, RWKV_COLS), dt)
    zero_wkv = jnp.zeros((DEPTH, b, RWKV_HEADS, RWKV_HEAD, RWKV_HEAD), dt)
    y_prompt, p_ckv, p_krope, p_pool, p_shift, p_wkv = run_trunk(
        x_prompt, pos_p, zero_pool, zero_shift, zero_wkv, [mla_attend_prompt] * DEPTH, layers)

    pos_s = PAST_LEN + jnp.arange(x_sample.shape[1], dtype=jnp.int32)
    sample_attends = [functools.partial(mla_attend_sample, cache_ckv, cache_krope, l, page_table)
                      for l in range(DEPTH)]
    y_sample, s_ckv, s_krope, s_pool, s_shift, s_wkv = run_trunk(
        x_sample, pos_s, state_pool, state_shift, state_wkv, sample_attends, layers)
    return (y_prompt, y_sample, p_ckv, p_krope, p_pool, p_shift, p_wkv, s_ckv, s_krope, s_pool, s_shift, s_wkv)
```

```python
import functools
import math

import numpy as np
import jax
import jax.numpy as jnp
from jax import lax
from jax.experimental import pallas as pl
from jax.experimental.pallas import tpu as pltpu

F32 = jnp.float32
BF16 = jnp.bfloat16

D_MODEL = 1024
POOL_WIDTH = 256
POOL_WINDOWS = (2, 4, 8, 16)
POOL_GDIM = 64
POOL_BUF = 15
RWKV_HEAD = 64
RWKV_WIDTH = 256
RWKV_HEADS = 4
RWKV_COLS = 832
RWKV_GN_EPS = 64e-5
MLA_HEADS = 8
MLA_V_DIM = 64
MLA_NOPE = 64
MLA_ROPE = 32
MLA_Q_RANK = 384
MLA_KV_RANK = 256
MLA_SCALE = (MLA_NOPE + MLA_ROPE) ** -0.5
ROPE_BASE = 10000.0
PAGE_SIZE = 128
N_EXPERTS = 8
DEPTH = 2
DEEPNORM_ALPHA = (2 * DEPTH) ** 0.25
LN_EPS = 1e-5
RMS_EPS = 1e-6

LANE = 128
KCAT = MLA_KV_RANK + LANE
PIN_PAD = 2048
NEG = -1e30
VMEM_LIMIT = 56 * 1024 * 1024

NN = (((1,), (0,)), ((), ()))
NT = (((1,), (1,)), ((), ()))
TN = (((0,), (0,)), ((), ()))


def _tile(n, pref):
    t = min(n, pref)
    while t > 8 and (n % t or t % 8):
        t -= 8
    assert n % t == 0, (n, pref)
    return t


def _cparams(sem):
    return pltpu.CompilerParams(dimension_semantics=sem, vmem_limit_bytes=VMEM_LIMIT)


def _full(shape):
    nd = len(shape)
    return pl.BlockSpec(shape, lambda *_: (0,) * nd)


def _split(x):
    hi = x.astype(BF16)
    lo = (x - hi.astype(F32)).astype(BF16)
    return hi, lo


def _dot3(a, b, dims=NN):
    ah, al = _split(a)
    bh, bl = _split(b)
    d = lambda p, q: lax.dot_general(p, q, dims, preferred_element_type=F32)
    return d(ah, bh) + d(ah, bl) + d(al, bh)


def _dotb(a, b, dims=NN):
    return lax.dot_general(a.astype(BF16), b.astype(BF16), dims, preferred_element_type=F32)


def _sigmoid(x):
    return 1.0 / (1.0 + jnp.exp(-x))


def _layer_norm(x, g, b):
    mu = jnp.mean(x, axis=-1, keepdims=True)
    xc = x - mu
    var = jnp.mean(xc * xc, axis=-1, keepdims=True)
    return xc * lax.rsqrt(var + LN_EPS) * g + b


def _proj_kernel(x_ref, win_ref, cos_ref, sin_ref, qg_ref, kvg_ref, wq_ref, wuk_ref,
                 pa_ref, pb_ref, ckv_ref, kr_ref, kcat_ref, qcat_ref):
    p = _dotb(x_ref[...], win_ref[...])
    pa_ref[...] = p[:, 0:256]
    pb_ref[...] = p[:, 1152:1152 + RWKV_COLS]
    qc = p[:, 256:640]
    kvc = p[:, 640:896]
    cos = cos_ref[...]
    sin = sin_ref[...]
    kr = p[:, 896:1024] * cos + p[:, 1024:1152] * sin
    ckv = kvc * lax.rsqrt(jnp.mean(kvc * kvc, axis=-1, keepdims=True) + RMS_EPS) * kvg_ref[...]
    ckv_ref[...] = ckv
    kr_ref[...] = kr[:, :MLA_ROPE]
    kcat_ref[...] = jnp.concatenate([ckv, kr], axis=-1).astype(BF16)
    qn = qc * lax.rsqrt(jnp.mean(qc * qc, axis=-1, keepdims=True) + RMS_EPS) * qg_ref[...]
    q = _dotb(qn, wq_ref[...])
    for h in range(MLA_HEADS):
        ql = _dotb(q[:, h * LANE:(h + 1) * LANE], wuk_ref[h])
        qr = (q[:, 1024 + h * LANE:1024 + (h + 1) * LANE] * cos
              + q[:, 2048 + h * LANE:2048 + (h + 1) * LANE] * sin)
        qcat_ref[h] = (jnp.concatenate([ql, qr], axis=-1) * MLA_SCALE).astype(BF16)


def _proj(x, cos, sin, w):
    n = x.shape[0]
    t = _tile(n, 256)
    row = lambda c: pl.BlockSpec((t, c), lambda i: (i, 0))
    return pl.pallas_call(
        _proj_kernel,
        grid=(n // t,),
        in_specs=[row(D_MODEL), _full((D_MODEL, PIN_PAD)), row(LANE), row(LANE),
                  _full((1, MLA_Q_RANK)), _full((1, MLA_KV_RANK)),
                  _full((MLA_Q_RANK, 3 * MLA_HEADS * LANE)), _full((MLA_HEADS, LANE, MLA_KV_RANK))],
        out_specs=[row(POOL_WIDTH), row(RWKV_COLS), row(MLA_KV_RANK), row(MLA_ROPE), row(KCAT),
                   pl.BlockSpec((MLA_HEADS, t, KCAT), lambda i: (0, i, 0))],
        out_shape=[jax.ShapeDtypeStruct((n, POOL_WIDTH), F32), jax.ShapeDtypeStruct((n, RWKV_COLS), F32),
                   jax.ShapeDtypeStruct((n, MLA_KV_RANK), F32), jax.ShapeDtypeStruct((n, MLA_ROPE), F32),
                   jax.ShapeDtypeStruct((n, KCAT), BF16), jax.ShapeDtypeStruct((MLA_HEADS, n, KCAT), BF16)],
        compiler_params=_cparams(("parallel",)),
        name="proj",
    )(x, w["win"], cos, sin, w["qg"], w["kvg"], w["wq"], w["wuk"])


def _pool_kernel(p_ref, w_ref, sc_ref, o_ref, e_ref, c2_ref, c4_ref, c8_ref, *, t, full_count):
    i = pl.program_id(1)
    n = t + 16

    @pl.when(i == 0)
    def _():
        e_ref[0:16, :] = jnp.zeros((16, POOL_WIDTH), F32)

    @pl.when(i > 0)
    def _():
        e_ref[0:16, :] = e_ref[t:n, :]

    p = p_ref[...]
    e_ref[16:n, :] = p
    c2_ref[1:n, :] = e_ref[1:n, :] + e_ref[0:n - 1, :]
    c4_ref[3:n, :] = c2_ref[3:n, :] + c2_ref[1:n - 2, :]
    c8_ref[7:n, :] = c4_ref[7:n, :] + c4_ref[3:n - 4, :]
    c16 = c8_ref[16:n, :] + c8_ref[8:n - 8, :]
    lane = lax.broadcasted_iota(jnp.int32, (t, POOL_WIDTH), 1)
    win = jnp.where(lane < 64, c2_ref[16:n, :],
                    jnp.where(lane < 128, c4_ref[16:n, :], jnp.where(lane < 192, c8_ref[16:n, :], c16)))
    wlen = jnp.where(lane < 64, 2, jnp.where(lane < 128, 4, jnp.where(lane < 192, 8, 16)))
    if full_count:
        cnt = wlen
    else:
        pos = i * t + lax.broadcasted_iota(jnp.int32, (t, POOL_WIDTH), 0)
        cnt = jnp.minimum(pos + 1, wlen)
    d = win / cnt.astype(F32) - p
    o_ref[...] = (_dotb(d, w_ref[...]) * sc_ref[...]).astype(BF16)


def _pool(pa, nseq, w, full_count):
    n = pa.shape[0]
    s = n // nseq
    t = _tile(s, 512)
    nt = s // t
    buf = pltpu.VMEM((t + 16, POOL_WIDTH), F32)
    return pl.pallas_call(
        functools.partial(_pool_kernel, t=t, full_count=full_count),
        grid=(nseq, nt),
        in_specs=[pl.BlockSpec((t, POOL_WIDTH), lambda b, i: (b * nt + i, 0)),
                  _full((POOL_WIDTH, POOL_WIDTH)), _full((1, POOL_WIDTH))],
        out_specs=pl.BlockSpec((t, POOL_WIDTH), lambda b, i: (b * nt + i, 0)),
        out_shape=jax.ShapeDtypeStruct((n, POOL_WIDTH), BF16),
        scratch_shapes=[buf, buf, buf, buf],
        compiler_params=_cparams(("arbitrary", "arbitrary")),
        name="pool",
    )(pa, w["pool_w"], w["pool_scale"])


def _rwkv_pre_kernel(*refs, t, has_vres):
    if has_vres:
        (pb_ref, mu_ref, wl_ref, w0_ref, a0_ref, kkp_ref, ka_ref, bd_ref, vf_ref, v0_ref, v1_ref, v2_ref,
         r_ref, lw_ref, k_ref, v_ref, kk_ref, a_ref, g_ref, e_ref) = refs
    else:
        (pb_ref, mu_ref, wl_ref, w0_ref, a0_ref, kkp_ref, ka_ref, bd_ref,
         r_ref, lw_ref, k_ref, v_ref, kk_ref, a_ref, g_ref, e_ref) = refs
    i = pl.program_id(1)

    @pl.when(i == 0)
    def _():
        e_ref[0:8, :] = jnp.zeros((8, RWKV_COLS), F32)

    @pl.when(i > 0)
    def _():
        e_ref[0:8, :] = e_ref[t:t + 8, :]

    pb = pb_ref[...]
    e_ref[8:t + 8, :] = pb
    prev = e_ref[7:t + 7, :]
    xs = pb + (prev - pb) * mu_ref[...]
    r = xs[:, 0:256]
    k = xs[:, 256:512]
    v = xs[:, 512:768]
    lo = xs[:, 768:832]
    lane = lax.broadcasted_iota(jnp.int32, lo.shape, 1)
    act = jnp.where(lane < 16, jnp.tanh(lo), jnp.where(lane < 32, lo, _sigmoid(lo)))
    lora = _dot3(act, wl_ref[...])
    z = -(w0_ref[...] + lora[:, 0:256])
    softplus = jnp.maximum(z, 0.0) + jnp.log(1.0 + jnp.exp(-jnp.abs(z)))
    lw_ref[...] = -jnp.exp(-softplus - 0.5)
    if has_vres:
        vf = vf_ref[...]
        mix = _dot3(_dot3(v, v1_ref[...]), v2_ref[...])
        v = v + (vf - v) * _sigmoid(v0_ref[...] + mix)
    a = _sigmoid(a0_ref[...] + lora[:, 256:512])
    kk = k * kkp_ref[...]
    ss = _dot3(kk * kk, bd_ref[...])
    kk = kk / jnp.maximum(jnp.sqrt(ss), 1e-12)
    r_ref[...] = r
    k_ref[...] = k * (1.0 + (a - 1.0) * ka_ref[...])
    v_ref[...] = v
    kk_ref[...] = kk
    a_ref[...] = a
    g_ref[...] = lora[:, 512:768]


def _rwkv_pre(pb, nseq, w, vfirst):
    n = pb.shape[0]
    s = n // nseq
    t = _tile(s, 256)
    nt = s // t
    has_vres = vfirst is not None
    row = lambda c: pl.BlockSpec((t, c), lambda b, i: (b * nt + i, 0))
    vec = _full((1, RWKV_WIDTH))
    in_specs = [row(RWKV_COLS), _full((1, RWKV_COLS)), _full((64, 768)), vec, vec, vec, vec,
                _full((RWKV_WIDTH, RWKV_WIDTH))]
    args = [pb, w["mu"], w["lora"], w["w0"], w["a0"], w["kkp"], w["ka"], w["bd"]]
    if has_vres:
        in_specs += [row(RWKV_WIDTH), vec, _full((RWKV_WIDTH, LANE)), _full((LANE, RWKV_WIDTH))]
        args += [vfirst, w["v0"], w["v1"], w["v2"]]
    return pl.pallas_call(
        functools.partial(_rwkv_pre_kernel, t=t, has_vres=has_vres),
        grid=(nseq, nt),
        in_specs=in_specs,
        out_specs=[row(RWKV_WIDTH)] * 7,
        out_shape=[jax.ShapeDtypeStruct((n, RWKV_WIDTH), F32)] * 7,
        scratch_shapes=[pltpu.VMEM((t + 8, RWKV_COLS), F32)],
        compiler_params=_cparams(("arbitrary", "arbitrary")),
        name="rwkv_pre",
    )(*args)


def _scan_chunk(r, lw, k, v, kk, a, st, c):
    hc = RWKV_HEADS * c
    lane = lax.broadcasted_iota(jnp.int32, (c, RWKV_WIDTH), 1)
    ri = lax.broadcasted_iota(jnp.int32, (c, c), 0)
    ci = lax.broadcasted_iota(jnp.int32, (c, c), 1)
    tril = jnp.where(ri >= ci, 1.0, 0.0).astype(F32)
    cum = jnp.dot(tril, lw, preferred_element_type=F32, precision=lax.Precision.HIGHEST)
    tot = cum[c - 1:c, :]
    e_neg = jnp.exp(-cum)
    at = -kk * jnp.exp(cum - lw)
    bt = kk * a * e_neg
    kt = k * e_neg
    rt = r * jnp.exp(cum)
    e_rem = jnp.exp(tot - cum)
    bh = kk * a * e_rem
    kh = k * e_rem

    def bd(x):
        return jnp.concatenate([jnp.where((lane >= h * RWKV_HEAD) & (lane < (h + 1) * RWKV_HEAD), x, 0.0)
                                for h in range(RWKV_HEADS)], axis=0)

    lst = jnp.concatenate([bd(at), bd(rt)], axis=0)
    rst = jnp.concatenate([bd(bt), bd(kt)], axis=0)
    g = _dot3(lst, rst, NT)
    gr = lax.broadcasted_iota(jnp.int32, (hc, hc), 0) % c
    gc = lax.broadcasted_iota(jnp.int32, (hc, hc), 1) % c
    strict = gc < gr
    incl = gc <= gr
    n_ab = jnp.where(strict, g[0:hc, 0:hc], 0.0)
    a_ak = jnp.where(strict, g[0:hc, hc:2 * hc], 0.0)
    a_rb = jnp.where(incl, g[hc:2 * hc, 0:hc], 0.0)
    a_rk = jnp.where(incl, g[hc:2 * hc, hc:2 * hc], 0.0)
    eye = (lax.broadcasted_iota(jnp.int32, (hc, hc), 0) == lax.broadcasted_iota(jnp.int32, (hc, hc), 1))
    inv = jnp.where(eye, 1.0, 0.0) + n_ab
    npow = n_ab
    for _ in range(int(math.log2(c)) - 1):
        npow = _dot3(npow, npow)
        inv = inv + _dot3(inv, npow)
    w12 = _dot3(lst, st)
    vr = jnp.concatenate([v[:, h * RWKV_HEAD:(h + 1) * RWKV_HEAD] for h in range(RWKV_HEADS)], axis=0)
    u = _dot3(inv, w12[0:hc] + _dot3(a_ak, vr))
    yr = w12[hc:2 * hc] + _dot3(a_rb, u) + _dot3(a_rk, vr)
    y = jnp.concatenate([yr[h * c:(h + 1) * c, :] for h in range(RWKV_HEADS)], axis=1)
    e256r = lax.broadcasted_iota(jnp.int32, (RWKV_WIDTH, RWKV_WIDTH), 0)
    e256c = lax.broadcasted_iota(jnp.int32, (RWKV_WIDTH, RWKV_WIDTH), 1)
    dec = jnp.where(e256r == e256c, jnp.broadcast_to(jnp.exp(tot), (RWKV_WIDTH, RWKV_WIDTH)), 0.0)
    lhs = jnp.concatenate([dec, bd(bh), bd(kh)], axis=0)
    rhs = jnp.concatenate([st, u, vr], axis=0)
    return y, _dot3(lhs, rhs, TN)


def _scan_kernel(r_ref, lw_ref, k_ref, v_ref, kk_ref, a_ref, s0_ref, y_ref, sf_ref, st_ref, *, c, nch):
    j = pl.program_id(1)

    @pl.when(j == 0)
    def _():
        st_ref[...] = s0_ref[0]

    def body(ch, carry):
        sl = pl.ds(pl.multiple_of(ch * c, c), c)
        y, st = _scan_chunk(r_ref[sl, :], lw_ref[sl, :], k_ref[sl, :], v_ref[sl, :], kk_ref[sl, :], a_ref[sl, :],
                            st_ref[...], c)
        y_ref[sl, :] = y
        st_ref[...] = st
        return carry

    lax.fori_loop(0, nch, body, 0)

    @pl.when(j == pl.num_programs(1) - 1)
    def _():
        sf_ref[0] = st_ref[...]


def _scan(r, lw, k, v, kk, a, s0, c):
    n = r.shape[0]
    nseq = s0.shape[0]
    s = n // nseq
    nch = max(1, min(8, s // c))
    while (s // c) % nch:
        nch -= 1
    t = c * nch
    nt = s // t
    row = pl.BlockSpec((t, RWKV_WIDTH), lambda b, i: (b * nt + i, 0))
    st_spec = pl.BlockSpec((1, RWKV_WIDTH, RWKV_HEAD), lambda b, i: (b, 0, 0))
    return pl.pallas_call(
        functools.partial(_scan_kernel, c=c, nch=nch),
        grid=(nseq, nt),
        in_specs=[row] * 6 + [st_spec],
        out_specs=[row, st_spec],
        out_shape=[jax.ShapeDtypeStruct((n, RWKV_WIDTH), F32),
                   jax.ShapeDtypeStruct((nseq, RWKV_WIDTH, RWKV_HEAD), F32)],
        scratch_shapes=[pltpu.VMEM((RWKV_WIDTH, RWKV_HEAD), F32)],
        compiler_params=_cparams(("arbitrary", "arbitrary")),
        name="wkv_scan",
    )(r, lw, k, v, kk, a, s0)


def _latent_to_heads(o, wuv_ref, tq):
    out = None
    for h in range(MLA_HEADS):
        part = _dotb(o[h * tq:(h + 1) * tq, :], wuv_ref[h])
        out = part if out is None else out + part
    return out


def _attn_prompt_kernel(q_ref, k_ref, wuv_ref, o_ref, m_ref, l_ref, acc_ref, *, tq, tk):
    qi = pl.program_id(1)
    ki = pl.program_id(2)
    rows = MLA_HEADS * tq

    @pl.when(ki == 0)
    def _():
        m_ref[...] = jnp.full((rows, 1), NEG, F32)
        l_ref[...] = jnp.zeros((rows, 1), F32)
        acc_ref[...] = jnp.zeros((rows, MLA_KV_RANK), F32)

    @pl.when(ki * tk <= qi * tq + tq - 1)
    def _():
        q = q_ref[...].reshape(rows, KCAT)
        k = k_ref[...]
        s = lax.dot_general(q, k, NT, preferred_element_type=F32)
        qpos = qi * tq + lax.broadcasted_iota(jnp.int32, (rows, tk), 0) % tq
        kpos = ki * tk + lax.broadcasted_iota(jnp.int32, (rows, tk), 1)
        s = jnp.where(kpos <= qpos, s, NEG)
        m_old = m_ref[...]
        m_new = jnp.maximum(m_old, jnp.max(s, axis=-1, keepdims=True))
        alpha = jnp.exp(m_old - m_new)
        p = jnp.exp(s - m_new)
        l_ref[...] = alpha * l_ref[...] + jnp.sum(p, axis=-1, keepdims=True)
        acc_ref[...] = alpha * acc_ref[...] + jnp.dot(p.astype(BF16), k[:, :MLA_KV_RANK],
                                                      preferred_element_type=F32)
        m_ref[...] = m_new

    @pl.when(ki == pl.num_programs(2) - 1)
    def _():
        o = acc_ref[...] / l_ref[...]
        o_ref[...] = _latent_to_heads(o, wuv_ref, tq).astype(BF16)


def _attn_prompt(qcat, kcat, wuv, nseq):
    n = kcat.shape[0]
    s = n // nseq
    tq = _tile(s, 128)
    tk = _tile(s, 512)
    nq, nk = s // tq, s // tk
    assert tq & (tq - 1) == 0
    last = lambda qi: (qi * tq + tq - 1) // tk
    return pl.pallas_call(
        functools.partial(_attn_prompt_kernel, tq=tq, tk=tk),
        grid=(nseq, nq, nk),
        in_specs=[pl.BlockSpec((MLA_HEADS, tq, KCAT), lambda b, qi, ki: (0, b * nq + qi, 0)),
                  pl.BlockSpec((tk, KCAT), lambda b, qi, ki: (b * nk + jnp.minimum(ki, last(qi)), 0)),
                  _full((MLA_HEADS, MLA_KV_RANK, MLA_HEADS * MLA_V_DIM))],
        out_specs=pl.BlockSpec((tq, MLA_HEADS * MLA_V_DIM), lambda b, qi, ki: (b * nq + qi, 0)),
        out_shape=jax.ShapeDtypeStruct((n, MLA_HEADS * MLA_V_DIM), BF16),
        scratch_shapes=[pltpu.VMEM((MLA_HEADS * tq, 1), F32), pltpu.VMEM((MLA_HEADS * tq, 1), F32),
                        pltpu.VMEM((MLA_HEADS * tq, MLA_KV_RANK), F32)],
        compiler_params=_cparams(("parallel", "parallel", "arbitrary")),
        name="attn_prompt",
    )(qcat, kcat, wuv)


def _attn_sample_kernel(pt_ref, q_ref, kn_ref, wuv_ref, ckv_hbm, kr_hbm, o_ref, cbuf, rbuf, sems,
                        *, layer, ppc, nchunk, sd):
    b = pl.program_id(0)
    nb = pl.num_programs(0)
    rows = MLA_HEADS * sd

    def copies(bb, chunk, slot):
        out = []
        for i in range(ppc):
            page = pt_ref[bb, chunk * ppc + i]
            out.append(pltpu.make_async_copy(ckv_hbm.at[layer, page], cbuf.at[slot, i], sems.at[0, slot]))
            out.append(pltpu.make_async_copy(kr_hbm.at[layer, page], rbuf.at[slot, i], sems.at[1, slot]))
        return out

    @pl.when(b == 0)
    def _():
        for cp in copies(0, 0, 0):
            cp.start()

    q = q_ref[...].reshape(rows, KCAT)
    ql = q[:, :MLA_KV_RANK]
    qr = q[:, MLA_KV_RANK:MLA_KV_RANK + MLA_ROPE]
    m = jnp.full((rows, 1), NEG, F32)
    l = jnp.zeros((rows, 1), F32)
    acc = jnp.zeros((rows, MLA_KV_RANK), F32)
    for c in range(nchunk):
        slot = c % 2
        if c + 1 < nchunk:
            for cp in copies(b, c + 1, 1 - slot):
                cp.start()
        else:
            @pl.when(b + 1 < nb)
            def _():
                for cp in copies(b + 1, 0, 1 - slot):
                    cp.start()
        for cp in copies(b, c, slot):
            cp.wait()
        kc = cbuf[slot].reshape(ppc * PAGE_SIZE, MLA_KV_RANK).astype(BF16)
        kr = rbuf[slot].reshape(ppc * PAGE_SIZE, MLA_ROPE).astype(BF16)
        s = (lax.dot_general(ql, kc, NT, preferred_element_type=F32)
             + lax.dot_general(qr, kr, NT, preferred_element_type=F32))
        m_new = jnp.maximum(m, jnp.max(s, axis=-1, keepdims=True))
        alpha = jnp.exp(m - m_new)
        p = jnp.exp(s - m_new)
        l = alpha * l + jnp.sum(p, axis=-1, keepdims=True)
        acc = alpha * acc + jnp.dot(p.astype(BF16), kc, preferred_element_type=F32)
        m = m_new
    kn = kn_ref[...]
    s = lax.dot_general(q, kn, NT, preferred_element_type=F32)
    qpos = lax.broadcasted_iota(jnp.int32, (rows, sd), 0) % sd
    kpos = lax.broadcasted_iota(jnp.int32, (rows, sd), 1)
    s = jnp.where(kpos <= qpos, s, NEG)
    m_new = jnp.maximum(m, jnp.max(s, axis=-1, keepdims=True))
    alpha = jnp.exp(m - m_new)
    p = jnp.exp(s - m_new)
    l = alpha * l + jnp.sum(p, axis=-1, keepdims=True)
    acc = alpha * acc + jnp.dot(p.astype(BF16), kn[:, :MLA_KV_RANK], preferred_element_type=F32)
    o_ref[...] = _latent_to_heads(acc / l, wuv_ref, sd).astype(BF16)


def _attn_sample(qcat, kcat, wuv, cache_ckv, cache_krope, page_table, layer):
    nb, n_pages = page_table.shape
    n = kcat.shape[0]
    sd = n // nb
    ppc = max(1, min(16, n_pages // 2))
    nchunk = n_pages // ppc
    assert n_pages % ppc == 0 and nchunk % 2 == 0 and sd % 8 == 0 and sd & (sd - 1) == 0
    grid_spec = pltpu.PrefetchScalarGridSpec(
        num_scalar_prefetch=1,
        grid=(nb,),
        in_specs=[pl.BlockSpec((MLA_HEADS, sd, KCAT), lambda b, pt: (0, b, 0)),
                  pl.BlockSpec((sd, KCAT), lambda b, pt: (b, 0)),
                  pl.BlockSpec((MLA_HEADS, MLA_KV_RANK, MLA_HEADS * MLA_V_DIM), lambda b, pt: (0, 0, 0)),
                  pl.BlockSpec(memory_space=pl.ANY), pl.BlockSpec(memory_space=pl.ANY)],
        out_specs=pl.BlockSpec((sd, MLA_HEADS * MLA_V_DIM), lambda b, pt: (b, 0)),
        scratch_shapes=[pltpu.VMEM((2, ppc, PAGE_SIZE, MLA_KV_RANK), F32),
                        pltpu.VMEM((2, ppc, PAGE_SIZE, MLA_ROPE), F32),
                        pltpu.SemaphoreType.DMA((2, 2))],
    )
    return pl.pallas_call(
        functools.partial(_attn_sample_kernel, layer=layer, ppc=ppc, nchunk=nchunk, sd=sd),
        grid_spec=grid_spec,
        out_shape=jax.ShapeDtypeStruct((n, MLA_HEADS * MLA_V_DIM), BF16),
        compiler_params=_cparams(("arbitrary",)),
        name="attn_sample",
    )(page_table, qcat, kcat, wuv, cache_ckv, cache_krope)


def _post_kernel(y_ref, r_ref, k_ref, v_ref, g_ref, ya_ref, yc_ref, x_ref, rk_ref, lg_ref, lb_ref, bd_ref,
                 wo_ref, g1_ref, b1_ref, o_ref):
    y = y_ref[...]
    bd = bd_ref[...]
    inv = 1.0 / RWKV_HEAD
    mu = _dot3(y, bd) * inv
    yc = y - mu
    var = _dot3(yc * yc, bd) * inv
    yn = yc * lax.rsqrt(var + RWKV_GN_EPS) * lg_ref[...] + lb_ref[...]
    bonus = _dot3(r_ref[...] * k_ref[...] * rk_ref[...], bd) * v_ref[...]
    yb = (yn + bonus) * g_ref[...]
    h = (jnp.dot(ya_ref[...], wo_ref[0:256, :], preferred_element_type=F32)
         + _dotb(yb, wo_ref[256:512, :])
         + jnp.dot(yc_ref[...], wo_ref[512:1024, :], preferred_element_type=F32))
    o_ref[...] = _layer_norm(DEEPNORM_ALPHA * x_ref[...] + h, g1_ref[...], b1_ref[...])


def _post(y, r, k, v, g, ya, yc, x, w):
    n = x.shape[0]
    t = _tile(n, 256)
    row = lambda c: pl.BlockSpec((t, c), lambda i: (i, 0))
    vec = _full((1, RWKV_WIDTH))
    vecd = _full((1, D_MODEL))
    return pl.pallas_call(
        _post_kernel,
        grid=(n // t,),
        in_specs=[row(256)] * 6 + [row(512), row(D_MODEL), vec, vec, vec, _full((256, 256)),
                                   _full((D_MODEL, D_MODEL)), vecd, vecd],
        out_specs=row(D_MODEL),
        out_shape=jax.ShapeDtypeStruct((n, D_MODEL), F32),
        compiler_params=_cparams(("parallel",)),
        name="post",
    )(y, r, k, v, g, ya, yc, x, w["rk"], w["lnx_g"], w["lnx_b"], w["bd"], w["wo"], w["ln1_g"], w["ln1_b"])


def _ffn_kernel(x_ref, wg_ref, wu_ref, wd_ref, g2_ref, b2_ref, o_ref, acc_ref):
    f = pl.program_id(1)

    @pl.when(f == 0)
    def _():
        acc_ref[...] = jnp.zeros(acc_ref.shape, F32)

    xb = x_ref[...].astype(BF16)
    g = jnp.dot(xb, wg_ref[...], preferred_element_type=F32)
    u = jnp.dot(xb, wu_ref[...], preferred_element_type=F32)
    hmid = g * _sigmoid(g) * u
    acc_ref[...] += jnp.dot(hmid.astype(BF16), wd_ref[...], preferred_element_type=F32)

    @pl.when(f == pl.num_programs(1) - 1)
    def _():
        o_ref[...] = _layer_norm(DEEPNORM_ALPHA * x_ref[...] + acc_ref[...], g2_ref[...], b2_ref[...])


def _ffn(x, w):
    n = x.shape[0]
    dff = w["ffn_g"].shape[1]
    tm = _tile(n, 1024)
    tf = 256
    vecd = _full((1, D_MODEL))
    return pl.pallas_call(
        _ffn_kernel,
        grid=(n // tm, dff // tf),
        in_specs=[pl.BlockSpec((tm, D_MODEL), lambda i, f: (i, 0)),
                  pl.BlockSpec((D_MODEL, tf), lambda i, f: (0, f)),
                  pl.BlockSpec((D_MODEL, tf), lambda i, f: (0, f)),
                  pl.BlockSpec((tf, D_MODEL), lambda i, f: (f, 0)), vecd, vecd],
        out_specs=pl.BlockSpec((tm, D_MODEL), lambda i, f: (i, 0)),
        out_shape=jax.ShapeDtypeStruct((n, D_MODEL), F32),
        scratch_shapes=[pltpu.VMEM((tm, D_MODEL), F32)],
        compiler_params=_cparams(("parallel", "arbitrary")),
        name="ffn",
    )(x, w["ffn_g"], w["ffn_u"], w["ffn_d"], w["ln2_g"], w["ln2_b"])


def _moe_kernel(x_ref, rt_ref, wg_ref, wu_ref, wd_ref, g2_ref, b2_ref, o_ref, acc_ref, gate_ref):
    e = pl.program_id(1)
    f = pl.program_id(2)
    first = (e == 0) & (f == 0)
    last = (e == pl.num_programs(1) - 1) & (f == pl.num_programs(2) - 1)

    @pl.when(first)
    def _():
        acc_ref[...] = jnp.zeros(acc_ref.shape, F32)
        logits = jnp.dot(x_ref[...], rt_ref[...], preferred_element_type=F32, precision=lax.Precision.HIGHEST)
        lane = lax.broadcasted_iota(jnp.int32, logits.shape, 1)
        logits = jnp.where(lane < N_EXPERTS, logits, NEG)
        m1 = jnp.max(logits, axis=-1, keepdims=True)
        i1 = jnp.min(jnp.where(logits == m1, lane, LANE), axis=-1, keepdims=True)
        rest = jnp.where(lane == i1, NEG, logits)
        m2 = jnp.max(rest, axis=-1, keepdims=True)
        i2 = jnp.min(jnp.where(rest == m2, lane, LANE), axis=-1, keepdims=True)
        e2 = jnp.exp(m2 - m1)
        g1 = 1.0 / (1.0 + e2)
        gate_ref[...] = jnp.where(lane == i1, g1, 0.0) + jnp.where(lane == i2, e2 * g1, 0.0)

    gates = gate_ref[...]
    lane = lax.broadcasted_iota(jnp.int32, gates.shape, 1)
    gcol = jnp.sum(jnp.where(lane == e, gates, 0.0), axis=-1, keepdims=True)
    xb = x_ref[...].astype(BF16)
    g = jnp.dot(xb, wg_ref[0], preferred_element_type=F32)
    u = jnp.dot(xb, wu_ref[0], preferred_element_type=F32)
    hmid = g * _sigmoid(g) * u * gcol
    acc_ref[...] += jnp.dot(hmid.astype(BF16), wd_ref[0], preferred_element_type=F32)

    @pl.when(last)
    def _():
        o_ref[...] = _layer_norm(DEEPNORM_ALPHA * x_ref[...] + acc_ref[...], g2_ref[...], b2_ref[...])


def _moe(x, w):
    n = x.shape[0]
    ne, _, dff = w["moe_g"].shape
    tm = _tile(n, 1024)
    tf = 512
    vecd = _full((1, D_MODEL))
    return pl.pallas_call(
        _moe_kernel,
        grid=(n // tm, ne, dff // tf),
        in_specs=[pl.BlockSpec((tm, D_MODEL), lambda i, e, f: (i, 0)),
                  _full((D_MODEL, LANE)),
                  pl.BlockSpec((1, D_MODEL, tf), lambda i, e, f: (e, 0, f)),
                  pl.BlockSpec((1, D_MODEL, tf), lambda i, e, f: (e, 0, f)),
                  pl.BlockSpec((1, tf, D_MODEL), lambda i, e, f: (e, f, 0)), vecd, vecd],
        out_specs=pl.BlockSpec((tm, D_MODEL), lambda i, e, f: (i, 0)),
        out_shape=jax.ShapeDtypeStruct((n, D_MODEL), F32),
        scratch_shapes=[pltpu.VMEM((tm, D_MODEL), F32), pltpu.VMEM((tm, LANE), F32)],
        compiler_params=_cparams(("parallel", "arbitrary", "arbitrary")),
        name="moe",
    )(x, w["router"], w["moe_g"], w["moe_u"], w["moe_d"], w["ln2_g"], w["ln2_b"])


_RWKV_PERM = np.concatenate([np.arange(0, 256), np.arange(272, 528), np.arange(528, 784),
                             np.arange(256, 272), np.arange(784, 800), np.arange(800, 832)])
_RWKV_INV = np.argsort(_RWKV_PERM)


def _rot_cols(wr):
    half = MLA_ROPE // 2
    return jnp.concatenate([-wr[..., half:], wr[..., :half]], axis=-1)


def _pad_last(x, width):
    return jnp.pad(x, [(0, 0)] * (x.ndim - 1) + [(0, width - x.shape[-1])])


def _prep_layer(l, p):
    w = {}
    w_in = p["w_in"][l]
    rw = w_in[:, 256:256 + RWKV_COLS][:, _RWKV_PERM]
    mla = w_in[:, 256 + RWKV_COLS:]
    wkr = mla[:, MLA_Q_RANK + MLA_KV_RANK:]
    win = jnp.concatenate([w_in[:, :256], mla[:, :MLA_Q_RANK + MLA_KV_RANK], _pad_last(wkr, LANE),
                           _pad_last(_rot_cols(wkr), LANE), rw], axis=1)
    w["win"] = _pad_last(win, PIN_PAD).astype(BF16)
    wuq = p["mla_wuq"][l]
    wq_rope = wuq[:, :, MLA_NOPE:]
    w["wq"] = jnp.concatenate([_pad_last(wuq[:, :, :MLA_NOPE], LANE).reshape(MLA_Q_RANK, -1),
                               _pad_last(wq_rope, LANE).reshape(MLA_Q_RANK, -1),
                               _pad_last(_rot_cols(wq_rope), LANE).reshape(MLA_Q_RANK, -1)], axis=1).astype(BF16)
    wuk = jnp.transpose(p["mla_wuk"][l], (1, 2, 0))
    w["wuk"] = jnp.pad(wuk, ((0, 0), (0, LANE - MLA_NOPE), (0, 0))).astype(BF16)
    wuv = jnp.transpose(p["mla_wuv"][l], (1, 0, 2))
    eye = jnp.eye(MLA_HEADS, dtype=F32)
    w["wuv"] = (wuv[:, :, None, :] * eye[:, None, :, None]).reshape(MLA_HEADS, MLA_KV_RANK, -1).astype(BF16)
    w["qg"] = p["mla_qnorm_g"][l][None]
    w["kvg"] = p["mla_kvnorm_g"][l][None]
    pw = p["pool_w"][l]
    eye4 = jnp.eye(4, dtype=F32)
    w["pool_w"] = (pw[:, :, None, :] * eye4[:, None, :, None]).reshape(POOL_WIDTH, POOL_WIDTH).astype(BF16)
    w["pool_scale"] = p["pool_scale"][l][None]
    w["mu"] = p["rwkv_mu"][l][_RWKV_PERM][None]
    lora = jnp.zeros((64, 768), F32)
    lora = lora.at[0:16, 0:256].set(p["rwkv_w2"][l]).at[16:32, 256:512].set(p["rwkv_a2"][l])
    w["lora"] = lora.at[32:64, 512:768].set(p["rwkv_g2"][l])
    for name, key in (("w0", "rwkv_w0"), ("a0", "rwkv_a0"), ("kkp", "rwkv_kk"), ("ka", "rwkv_ka"),
                      ("lnx_g", "rwkv_lnx_g"), ("lnx_b", "rwkv_lnx_b")):
        w[name] = p[key][l][None]
    w["rk"] = p["rwkv_rk"][l].reshape(1, RWKV_WIDTH)
    w["bd"] = jnp.kron(jnp.eye(RWKV_HEADS, dtype=F32), jnp.ones((RWKV_HEAD, RWKV_HEAD), F32))
    if l > 0:
        w["v0"] = p["vres_v0"][l - 1][None]
        w["v1"] = _pad_last(p["vres_w1"][l - 1], LANE)
        w["v2"] = jnp.pad(p["vres_w2"][l - 1], ((0, LANE - p["vres_w2"].shape[1]), (0, 0)))
    w["wo"] = p["w_out"][l].astype(BF16)
    for name in ("ln1_g", "ln1_b", "ln2_g", "ln2_b"):
        w[name] = p[name][l][None]
    if l % 2 == 0:
        w["ffn_g"] = p["ffn_w_gate"][l // 2].astype(BF16)
        w["ffn_u"] = p["ffn_w_up"][l // 2].astype(BF16)
        w["ffn_d"] = p["ffn_w_down"][l // 2].astype(BF16)
    else:
        w["router"] = _pad_last(p["moe_router"][l // 2], LANE)
        w["moe_g"] = p["moe_w_gate"][l // 2].astype(BF16)
        w["moe_u"] = p["moe_w_up"][l // 2].astype(BF16)
        w["moe_d"] = p["moe_w_down"][l // 2].astype(BF16)
    return w


def _rope_tables(pos):
    half = MLA_ROPE // 2
    freqs = ROPE_BASE ** (-jnp.arange(half, dtype=F32) / half)
    ang = pos.astype(F32)[:, None] * freqs[None, :]
    cos = jnp.cos(ang)
    sin = jnp.sin(ang)
    return (_pad_last(jnp.concatenate([cos, cos], axis=-1), LANE),
            _pad_last(jnp.concatenate([sin, sin], axis=-1), LANE))


def _state_to_kernel(s):
    return jnp.swapaxes(s, -1, -2).reshape(s.shape[0], RWKV_WIDTH, RWKV_HEAD)


def _state_from_kernel(s):
    return jnp.swapaxes(s.reshape(s.shape[0], RWKV_HEADS, RWKV_HEAD, RWKV_HEAD), -1, -2)


def _channel_mix(x, l, w):
    return _ffn(x, w) if l % 2 == 0 else _moe(x, w)


def _run_prompt(x, layers):
    b, s, _ = x.shape
    xf = x.reshape(b * s, D_MODEL)
    cos, sin = _rope_tables(jnp.tile(jnp.arange(s, dtype=jnp.int32), b))
    vfirst = None
    outs = []
    for l, w in enumerate(layers):
        pa, pb, ckv, krope, kcat, qcat = _proj(xf, cos, sin, w)
        ya = _pool(pa, b, w, full_count=False)
        r, lw, k, v, kk, a, g = _rwkv_pre(pb, b, w, vfirst)
        if vfirst is None:
            vfirst = v
        y, sfin = _scan(r, lw, k, v, kk, a, jnp.zeros((b, RWKV_WIDTH, RWKV_HEAD), F32), c=_tile(s, 64))
        yc = _attn_prompt(qcat, kcat, w["wuv"], b)
        x1 = _post(y, r, k, v, g, ya, yc, xf, w)
        xf = _channel_mix(x1, l, w)
        outs.append((ckv.reshape(b, s, -1), krope.reshape(b, s, -1), pa.reshape(b, s, -1)[:, -POOL_BUF:],
                     pb.reshape(b, s, -1)[:, -1][:, _RWKV_INV], _state_from_kernel(sfin)))
    return (xf.reshape(b, s, D_MODEL),) + tuple(jnp.stack([o[i] for o in outs]) for i in range(5))


def _run_sample(x, past_len, cache_ckv, cache_krope, state_pool, state_shift, state_wkv, page_table, layers):
    b, s, _ = x.shape
    xf = x.reshape(b * s, D_MODEL)
    cos, sin = _rope_tables(jnp.tile(past_len + jnp.arange(s, dtype=jnp.int32), b))
    pool_lead = -(POOL_BUF + s) % 8
    shift_lead = 7
    vfirst = None
    outs = []
    for l, w in enumerate(layers):
        pa, pb, ckv, krope, kcat, qcat = _proj(xf, cos, sin, w)
        ext = jnp.concatenate([state_pool[l], pa.reshape(b, s, -1)], axis=1)
        ext_p = jnp.pad(ext, ((0, 0), (pool_lead, 0), (0, 0)))
        ya = _pool(ext_p.reshape(-1, POOL_WIDTH), 1, w, full_count=True)
        ya = ya.reshape(b, -1, POOL_WIDTH)[:, -s:].reshape(b * s, POOL_WIDTH)
        pbe = jnp.concatenate([state_shift[l][:, None, _RWKV_PERM], pb.reshape(b, s, -1)], axis=1)
        pbe = jnp.pad(pbe, ((0, 0), (shift_lead, 0), (0, 0)))
        pre = _rwkv_pre(pbe.reshape(-1, RWKV_COLS), 1, w, vfirst)
        if vfirst is None:
            vfirst = pre[3]
        r, lw, k, v, kk, a, g = (t.reshape(b, -1, RWKV_WIDTH)[:, -s:].reshape(b * s, RWKV_WIDTH) for t in pre)
        y, sfin = _scan(r, lw, k, v, kk, a, _state_to_kernel(state_wkv[l]), c=s)
        yc = _attn_sample(qcat, kcat, w["wuv"], cache_ckv, cache_krope, page_table, l)
        x1 = _post(y, r, k, v, g, ya, yc, xf, w)
        xf = _channel_mix(x1, l, w)
        outs.append((ckv.reshape(b, s, -1), krope.reshape(b, s, -1), ext[:, -POOL_BUF:],
                     pb.reshape(b, s, -1)[:, -1][:, _RWKV_INV], _state_from_kernel(sfin)))
    return (xf.reshape(b, s, D_MODEL),) + tuple(jnp.stack([o[i] for o in outs]) for i in range(5))


def kernel(x_prompt, x_sample, cache_ckv, cache_krope, state_pool, state_shift, state_wkv, page_table, ln1_g, ln1_b, ln2_g, ln2_b, w_in, pool_w, pool_scale, rwkv_mu, rwkv_w0, rwkv_w2, rwkv_a0, rwkv_a2, rwkv_g2, rwkv_kk, rwkv_ka, rwkv_rk, rwkv_lnx_g, rwkv_lnx_b, vres_v0, vres_w1, vres_w2, mla_qnorm_g, mla_wuq, mla_kvnorm_g, mla_wuk, mla_wuv, w_out, ffn_w_gate, ffn_w_up, ffn_w_down, moe_router, moe_w_gate, moe_w_up, moe_w_down):
    p = dict(ln1_g=ln1_g, ln1_b=ln1_b, ln2_g=ln2_g, ln2_b=ln2_b, w_in=w_in, pool_w=pool_w, pool_scale=pool_scale,
             rwkv_mu=rwkv_mu, rwkv_w0=rwkv_w0, rwkv_w2=rwkv_w2, rwkv_a0=rwkv_a0, rwkv_a2=rwkv_a2, rwkv_g2=rwkv_g2,
             rwkv_kk=rwkv_kk, rwkv_ka=rwkv_ka, rwkv_rk=rwkv_rk, rwkv_lnx_g=rwkv_lnx_g, rwkv_lnx_b=rwkv_lnx_b,
             vres_v0=vres_v0, vres_w1=vres_w1, vres_w2=vres_w2, mla_qnorm_g=mla_qnorm_g, mla_wuq=mla_wuq,
             mla_kvnorm_g=mla_kvnorm_g, mla_wuk=mla_wuk, mla_wuv=mla_wuv, w_out=w_out, ffn_w_gate=ffn_w_gate,
             ffn_w_up=ffn_w_up, ffn_w_down=ffn_w_down, moe_router=moe_router, moe_w_gate=moe_w_gate,
             moe_w_up=moe_w_up, moe_w_down=moe_w_down)
    layers = [_prep_layer(l, p) for l in range(DEPTH)]
    past_len = page_table.shape[1] * PAGE_SIZE
    prompt = _run_prompt(x_prompt, layers)
    sample = _run_sample(x_sample, past_len, cache_ckv, cache_krope, state_pool, state_shift, state_wkv,
                         page_table, layers)
    return (prompt[0], sample[0]) + prompt[1:] + sample[1:]
```

```python
import functools
import math

import numpy as np
import jax
import jax.numpy as jnp
from jax import lax
from jax.experimental import pallas as pl
from jax.experimental.pallas import tpu as pltpu

F32 = jnp.float32
BF16 = jnp.bfloat16

D_MODEL = 1024
POOL_WIDTH = 256
POOL_WINDOWS = (2, 4, 8, 16)
POOL_GDIM = 64
POOL_BUF = 15
RWKV_HEAD = 64
RWKV_WIDTH = 256
RWKV_HEADS = 4
RWKV_COLS = 832
RWKV_GN_EPS = 64e-5
MLA_HEADS = 8
MLA_V_DIM = 64
MLA_NOPE = 64
MLA_ROPE = 32
MLA_Q_RANK = 384
MLA_KV_RANK = 256
MLA_SCALE = (MLA_NOPE + MLA_ROPE) ** -0.5
QK_SCALE = MLA_SCALE * math.log2(math.e)
ROPE_BASE = 10000.0
PAGE_SIZE = 128
N_EXPERTS = 8
DEPTH = 2
DEEPNORM_ALPHA = (2 * DEPTH) ** 0.25
LN_EPS = 1e-5
RMS_EPS = 1e-6

LANE = 128
KCAT = MLA_KV_RANK + LANE
PIN_PAD = 2048
NEG = -1e30
VMEM_LIMIT = 56 * 1024 * 1024

NN = (((1,), (0,)), ((), ()))
NT = (((1,), (1,)), ((), ()))
TN = (((0,), (0,)), ((), ()))


def _tile(n, pref):
    t = min(n, pref)
    while t > 8 and (n % t or t % 8):
        t -= 8
    assert n % t == 0, (n, pref)
    return t


def _cparams(sem):
    return pltpu.CompilerParams(dimension_semantics=sem, vmem_limit_bytes=VMEM_LIMIT)


def _full(shape):
    nd = len(shape)
    return pl.BlockSpec(shape, lambda *_: (0,) * nd)


def _split(x):
    hi = x.astype(BF16)
    lo = (x - hi.astype(F32)).astype(BF16)
    return hi, lo


def _dot3(a, b, dims=NN):
    ah, al = _split(a)
    bh, bl = _split(b)
    d = lambda p, q: lax.dot_general(p, q, dims, preferred_element_type=F32)
    return d(ah, bh) + d(ah, bl) + d(al, bh)


def _dotb(a, b, dims=NN):
    return lax.dot_general(a.astype(BF16), b.astype(BF16), dims, preferred_element_type=F32)


def _sigmoid(x):
    return 1.0 / (1.0 + jnp.exp(-x))


def _layer_norm(x, g, b):
    mu = jnp.mean(x, axis=-1, keepdims=True)
    xc = x - mu
    var = jnp.mean(xc * xc, axis=-1, keepdims=True)
    return xc * lax.rsqrt(var + LN_EPS) * g + b


def _proj_kernel(x_ref, win_ref, cos_ref, sin_ref, qg_ref, kvg_ref, wq_ref, wuk_ref,
                 pa_ref, pb_ref, ckv_ref, kr_ref, kcat_ref, qcat_ref):
    p = _dotb(x_ref[...], win_ref[...])
    pa_ref[...] = p[:, 0:256]
    pb_ref[...] = p[:, 1152:1152 + RWKV_COLS]
    qc = p[:, 256:640]
    kvc = p[:, 640:896]
    cos = cos_ref[...]
    sin = sin_ref[...]
    kr = p[:, 896:1024] * cos + p[:, 1024:1152] * sin
    ckv = kvc * lax.rsqrt(jnp.mean(kvc * kvc, axis=-1, keepdims=True) + RMS_EPS) * kvg_ref[...]
    ckv_ref[...] = ckv
    kr_ref[...] = kr[:, :MLA_ROPE]
    kcat_ref[...] = jnp.concatenate([ckv, kr], axis=-1).astype(BF16)
    qn = qc * lax.rsqrt(jnp.mean(qc * qc, axis=-1, keepdims=True) + RMS_EPS) * qg_ref[...]
    q = _dotb(qn, wq_ref[...])
    for h in range(MLA_HEADS):
        ql = _dotb(q[:, h * LANE:(h + 1) * LANE], wuk_ref[h])
        qr = (q[:, 1024 + h * LANE:1024 + (h + 1) * LANE] * cos
              + q[:, 2048 + h * LANE:2048 + (h + 1) * LANE] * sin)
        qcat_ref[h] = (jnp.concatenate([ql, qr], axis=-1) * QK_SCALE).astype(BF16)


def _proj(x, cos, sin, w):
    n = x.shape[0]
    t = _tile(n, 256)
    row = lambda c: pl.BlockSpec((t, c), lambda i: (i, 0))
    return pl.pallas_call(
        _proj_kernel,
        grid=(n // t,),
        in_specs=[row(D_MODEL), _full((D_MODEL, PIN_PAD)), row(LANE), row(LANE),
                  _full((1, MLA_Q_RANK)), _full((1, MLA_KV_RANK)),
                  _full((MLA_Q_RANK, 3 * MLA_HEADS * LANE)), _full((MLA_HEADS, LANE, MLA_KV_RANK))],
        out_specs=[row(POOL_WIDTH), row(RWKV_COLS), row(MLA_KV_RANK), row(MLA_ROPE), row(KCAT),
                   pl.BlockSpec((MLA_HEADS, t, KCAT), lambda i: (0, i, 0))],
        out_shape=[jax.ShapeDtypeStruct((n, POOL_WIDTH), F32), jax.ShapeDtypeStruct((n, RWKV_COLS), F32),
                   jax.ShapeDtypeStruct((n, MLA_KV_RANK), F32), jax.ShapeDtypeStruct((n, MLA_ROPE), F32),
                   jax.ShapeDtypeStruct((n, KCAT), BF16), jax.ShapeDtypeStruct((MLA_HEADS, n, KCAT), BF16)],
        compiler_params=_cparams(("parallel",)),
        name="proj",
    )(x, w["win"], cos, sin, w["qg"], w["kvg"], w["wq"], w["wuk"])


def _pool_kernel(p_ref, w_ref, sc_ref, o_ref, e_ref, c2_ref, c4_ref, c8_ref, *, t, full_count):
    i = pl.program_id(1)
    n = t + 16

    @pl.when(i == 0)
    def _():
        e_ref[0:16, :] = jnp.zeros((16, POOL_WIDTH), F32)

    @pl.when(i > 0)
    def _():
        e_ref[0:16, :] = e_ref[t:n, :]

    p = p_ref[...]
    e_ref[16:n, :] = p
    c2_ref[1:n, :] = e_ref[1:n, :] + e_ref[0:n - 1, :]
    c4_ref[3:n, :] = c2_ref[3:n, :] + c2_ref[1:n - 2, :]
    c8_ref[7:n, :] = c4_ref[7:n, :] + c4_ref[3:n - 4, :]
    c16 = c8_ref[16:n, :] + c8_ref[8:n - 8, :]
    lane = lax.broadcasted_iota(jnp.int32, (t, POOL_WIDTH), 1)
    win = jnp.where(lane < 64, c2_ref[16:n, :],
                    jnp.where(lane < 128, c4_ref[16:n, :], jnp.where(lane < 192, c8_ref[16:n, :], c16)))
    wlen = jnp.where(lane < 64, 2, jnp.where(lane < 128, 4, jnp.where(lane < 192, 8, 16)))
    if full_count:
        cnt = wlen
    else:
        pos = i * t + lax.broadcasted_iota(jnp.int32, (t, POOL_WIDTH), 0)
        cnt = jnp.minimum(pos + 1, wlen)
    d = win / cnt.astype(F32) - p
    o_ref[...] = (_dotb(d, w_ref[...]) * sc_ref[...]).astype(BF16)


def _pool(pa, nseq, w, full_count):
    n = pa.shape[0]
    s = n // nseq
    t = _tile(s, 512)
    nt = s // t
    buf = pltpu.VMEM((t + 16, POOL_WIDTH), F32)
    return pl.pallas_call(
        functools.partial(_pool_kernel, t=t, full_count=full_count),
        grid=(nseq, nt),
        in_specs=[pl.BlockSpec((t, POOL_WIDTH), lambda b, i: (b * nt + i, 0)),
                  _full((POOL_WIDTH, POOL_WIDTH)), _full((1, POOL_WIDTH))],
        out_specs=pl.BlockSpec((t, POOL_WIDTH), lambda b, i: (b * nt + i, 0)),
        out_shape=jax.ShapeDtypeStruct((n, POOL_WIDTH), BF16),
        scratch_shapes=[buf, buf, buf, buf],
        compiler_params=_cparams(("arbitrary", "arbitrary")),
        name="pool",
    )(pa, w["pool_w"], w["pool_scale"])


def _rwkv_pre_kernel(*refs, t, has_vres):
    if has_vres:
        (pb_ref, mu_ref, wl_ref, w0_ref, a0_ref, kkp_ref, ka_ref, bd_ref, vf_ref, v0_ref, v1_ref, v2_ref,
         r_ref, lw_ref, k_ref, v_ref, kk_ref, a_ref, g_ref, e_ref) = refs
    else:
        (pb_ref, mu_ref, wl_ref, w0_ref, a0_ref, kkp_ref, ka_ref, bd_ref,
         r_ref, lw_ref, k_ref, v_ref, kk_ref, a_ref, g_ref, e_ref) = refs
    i = pl.program_id(1)

    @pl.when(i == 0)
    def _():
        e_ref[0:8, :] = jnp.zeros((8, RWKV_COLS), F32)

    @pl.when(i > 0)
    def _():
        e_ref[0:8, :] = e_ref[t:t + 8, :]

    pb = pb_ref[...]
    e_ref[8:t + 8, :] = pb
    prev = e_ref[7:t + 7, :]
    xs = pb + (prev - pb) * mu_ref[...]
    r = xs[:, 0:256]
    k = xs[:, 256:512]
    v = xs[:, 512:768]
    lo = xs[:, 768:832]
    lane = lax.broadcasted_iota(jnp.int32, lo.shape, 1)
    act = jnp.where(lane < 16, jnp.tanh(lo), jnp.where(lane < 32, lo, _sigmoid(lo)))
    lora = _dot3(act, wl_ref[...])
    z = -(w0_ref[...] + lora[:, 0:256])
    softplus = jnp.maximum(z, 0.0) + jnp.log(1.0 + jnp.exp(-jnp.abs(z)))
    lw_ref[...] = -jnp.exp(-softplus - 0.5)
    if has_vres:
        vf = vf_ref[...]
        mix = _dot3(_dot3(v, v1_ref[...]), v2_ref[...])
        v = v + (vf - v) * _sigmoid(v0_ref[...] + mix)
    a = _sigmoid(a0_ref[...] + lora[:, 256:512])
    kk = k * kkp_ref[...]
    ss = _dot3(kk * kk, bd_ref[...])
    kk = kk / jnp.maximum(jnp.sqrt(ss), 1e-12)
    r_ref[...] = r
    k_ref[...] = k * (1.0 + (a - 1.0) * ka_ref[...])
    v_ref[...] = v
    kk_ref[...] = kk
    a_ref[...] = a
    g_ref[...] = lora[:, 512:768]


def _rwkv_pre(pb, nseq, w, vfirst):
    n = pb.shape[0]
    s = n // nseq
    t = _tile(s, 256)
    nt = s // t
    has_vres = vfirst is not None
    row = lambda c: pl.BlockSpec((t, c), lambda b, i: (b * nt + i, 0))
    vec = _full((1, RWKV_WIDTH))
    in_specs = [row(RWKV_COLS), _full((1, RWKV_COLS)), _full((64, 768)), vec, vec, vec, vec,
                _full((RWKV_WIDTH, RWKV_WIDTH))]
    args = [pb, w["mu"], w["lora"], w["w0"], w["a0"], w["kkp"], w["ka"], w["bd"]]
    if has_vres:
        in_specs += [row(RWKV_WIDTH), vec, _full((RWKV_WIDTH, LANE)), _full((LANE, RWKV_WIDTH))]
        args += [vfirst, w["v0"], w["v1"], w["v2"]]
    return pl.pallas_call(
        functools.partial(_rwkv_pre_kernel, t=t, has_vres=has_vres),
        grid=(nseq, nt),
        in_specs=in_specs,
        out_specs=[row(RWKV_WIDTH)] * 7,
        out_shape=[jax.ShapeDtypeStruct((n, RWKV_WIDTH), F32)] * 7,
        scratch_shapes=[pltpu.VMEM((t + 8, RWKV_COLS), F32)],
        compiler_params=_cparams(("arbitrary", "arbitrary")),
        name="rwkv_pre",
    )(*args)


def _scan_chunk(r, lw, k, v, kk, a, st, c):
    hc = RWKV_HEADS * c
    lane = lax.broadcasted_iota(jnp.int32, (c, RWKV_WIDTH), 1)
    ri = lax.broadcasted_iota(jnp.int32, (c, c), 0)
    ci = lax.broadcasted_iota(jnp.int32, (c, c), 1)
    tril = jnp.where(ri >= ci, 1.0, 0.0).astype(F32)
    cum = jnp.dot(tril, lw, preferred_element_type=F32, precision=lax.Precision.HIGHEST)
    tot = cum[c - 1:c, :]
    e_neg = jnp.exp(-cum)
    at = -kk * jnp.exp(cum - lw)
    bt = kk * a * e_neg
    kt = k * e_neg
    rt = r * jnp.exp(cum)
    e_rem = jnp.exp(tot - cum)
    bh = kk * a * e_rem
    kh = k * e_rem

    def bd(x):
        return jnp.concatenate([jnp.where((lane >= h * RWKV_HEAD) & (lane < (h + 1) * RWKV_HEAD), x, 0.0)
                                for h in range(RWKV_HEADS)], axis=0)

    lst = jnp.concatenate([bd(at), bd(rt)], axis=0)
    rst = jnp.concatenate([bd(bt), bd(kt)], axis=0)
    g = _dot3(lst, rst, NT)
    gr = lax.broadcasted_iota(jnp.int32, (hc, hc), 0) % c
    gc = lax.broadcasted_iota(jnp.int32, (hc, hc), 1) % c
    strict = gc < gr
    incl = gc <= gr
    n_ab = jnp.where(strict, g[0:hc, 0:hc], 0.0)
    a_ak = jnp.where(strict, g[0:hc, hc:2 * hc], 0.0)
    a_rb = jnp.where(incl, g[hc:2 * hc, 0:hc], 0.0)
    a_rk = jnp.where(incl, g[hc:2 * hc, hc:2 * hc], 0.0)
    eye = (lax.broadcasted_iota(jnp.int32, (hc, hc), 0) == lax.broadcasted_iota(jnp.int32, (hc, hc), 1))
    inv = jnp.where(eye, 1.0, 0.0) + n_ab
    npow = n_ab
    for _ in range(int(math.log2(c)) - 1):
        npow = _dot3(npow, npow)
        inv = inv + _dot3(inv, npow)
    w12 = _dot3(lst, st)
    vr = jnp.concatenate([v[:, h * RWKV_HEAD:(h + 1) * RWKV_HEAD] for h in range(RWKV_HEADS)], axis=0)
    u = _dot3(inv, w12[0:hc] + _dot3(a_ak, vr))
    yr = w12[hc:2 * hc] + _dot3(a_rb, u) + _dot3(a_rk, vr)
    y = jnp.concatenate([yr[h * c:(h + 1) * c, :] for h in range(RWKV_HEADS)], axis=1)
    e256r = lax.broadcasted_iota(jnp.int32, (RWKV_WIDTH, RWKV_WIDTH), 0)
    e256c = lax.broadcasted_iota(jnp.int32, (RWKV_WIDTH, RWKV_WIDTH), 1)
    dec = jnp.where(e256r == e256c, jnp.broadcast_to(jnp.exp(tot), (RWKV_WIDTH, RWKV_WIDTH)), 0.0)
    lhs = jnp.concatenate([dec, bd(bh), bd(kh)], axis=0)
    rhs = jnp.concatenate([st, u, vr], axis=0)
    return y, _dot3(lhs, rhs, TN)


def _scan_kernel(r_ref, lw_ref, k_ref, v_ref, kk_ref, a_ref, s0_ref, y_ref, sf_ref, st_ref, *, c, nch, nb):
    j = pl.program_id(1)

    @pl.when(j == 0)
    def _():
        st_ref[...] = s0_ref[...]

    def body(ch, carry):
        sl = pl.ds(pl.multiple_of(ch * c, c), c)
        for i in range(nb):
            y, st = _scan_chunk(r_ref[i, sl, :], lw_ref[i, sl, :], k_ref[i, sl, :], v_ref[i, sl, :],
                                kk_ref[i, sl, :], a_ref[i, sl, :], st_ref[i], c)
            y_ref[i, sl, :] = y
            st_ref[i] = st
        return carry

    lax.fori_loop(0, nch, body, 0)

    @pl.when(j == pl.num_programs(1) - 1)
    def _():
        sf_ref[...] = st_ref[...]


def _scan(r, lw, k, v, kk, a, s0, c):
    n = r.shape[0]
    nseq = s0.shape[0]
    s = n // nseq
    nb = 2 if s > c else math.gcd(nseq, 8)
    nch = max(1, min(8, s // c))
    while (s // c) % nch:
        nch -= 1
    t = c * nch
    nt = s // t
    row = pl.BlockSpec((nb, t, RWKV_WIDTH), lambda b, i: (b, i, 0))
    st_spec = pl.BlockSpec((nb, RWKV_WIDTH, RWKV_HEAD), lambda b, i: (b, 0, 0))
    seqs = [x.reshape(nseq, s, RWKV_WIDTH) for x in (r, lw, k, v, kk, a)]
    y, sfin = pl.pallas_call(
        functools.partial(_scan_kernel, c=c, nch=nch, nb=nb),
        grid=(nseq // nb, nt),
        in_specs=[row] * 6 + [st_spec],
        out_specs=[row, st_spec],
        out_shape=[jax.ShapeDtypeStruct((nseq, s, RWKV_WIDTH), F32),
                   jax.ShapeDtypeStruct((nseq, RWKV_WIDTH, RWKV_HEAD), F32)],
        scratch_shapes=[pltpu.VMEM((nb, RWKV_WIDTH, RWKV_HEAD), F32)],
        compiler_params=_cparams(("arbitrary", "arbitrary")),
        name="wkv_scan",
    )(*seqs, s0)
    return y.reshape(n, RWKV_WIDTH), sfin


def _latent_to_heads(o, wuv_ref, tq):
    out = None
    for h in range(MLA_HEADS):
        part = _dotb(o[h * tq:(h + 1) * tq, :], wuv_ref[h])
        out = part if out is None else out + part
    return out


CHAIN_ROWS = 1024


def _attn_prompt_kernel(qi_ref, ki_ref, q_ref, k_ref, wuv_ref, o_ref, m_ref, l_ref, acc_ref, *, tq, tk):
    step = pl.program_id(1)
    qi = qi_ref[step]
    ki = ki_ref[step]
    rows = MLA_HEADS * tq

    @pl.when(ki == 0)
    def _():
        m_ref[...] = jnp.full((rows, 1), NEG, F32)
        l_ref[...] = jnp.zeros((rows, LANE), F32)
        acc_ref[...] = jnp.zeros((rows, MLA_KV_RANK), F32)

    last = (qi * tq + tq - 1) // tk
    cheads = max(1, CHAIN_ROWS // tq)
    crows = cheads * tq

    def update(masked):
        k = k_ref[...]
        kv = k[:, :MLA_KV_RANK]
        if masked:
            rel = (lax.broadcasted_iota(jnp.int32, (crows, tk), 1)
                   - lax.broadcasted_iota(jnp.int32, (crows, tk), 0) % tq)
            visible = rel <= qi * tq - ki * tk
        for c in range(MLA_HEADS // cheads):
            r0 = c * crows
            q = q_ref[c * cheads:(c + 1) * cheads].reshape(crows, KCAT)
            s = lax.dot_general(q, k, NT, preferred_element_type=F32)
            if masked:
                s = jnp.where(visible, s, NEG)
            m_old = m_ref[r0:r0 + crows, :]
            m_new = jnp.maximum(m_old, jnp.max(s, axis=-1, keepdims=True))
            alpha = jnp.exp2(m_old - m_new)
            p = jnp.exp2(s - m_new)
            psum = p[:, 0:LANE]
            for t in range(1, tk // LANE):
                psum = psum + p[:, t * LANE:(t + 1) * LANE]
            l_ref[r0:r0 + crows, :] = alpha * l_ref[r0:r0 + crows, :] + psum
            acc_ref[r0:r0 + crows, :] = alpha * acc_ref[r0:r0 + crows, :] + jnp.dot(
                p.astype(BF16), kv, preferred_element_type=F32)
            m_ref[r0:r0 + crows, :] = m_new

    @pl.when(ki < last)
    def _():
        update(False)

    @pl.when(ki == last)
    def _():
        update(True)
        o = acc_ref[...] / jnp.sum(l_ref[...], axis=-1, keepdims=True)
        o_ref[...] = _latent_to_heads(o, wuv_ref, tq).astype(BF16)


def _attn_prompt(qcat, kcat, wuv, nseq):
    n = kcat.shape[0]
    s = n // nseq
    tq = _tile(s, 512)
    tk = _tile(s, 512)
    nq, nk = s // tq, s // tk
    assert tk % LANE == 0 and tk % tq == 0
    pairs = [(qi, ki) for qi in range(nq) for ki in range((qi * tq + tq - 1) // tk + 1)]
    qi_tab = jnp.asarray(np.array([p[0] for p in pairs], np.int32))
    ki_tab = jnp.asarray(np.array([p[1] for p in pairs], np.int32))
    rows = MLA_HEADS * tq
    grid_spec = pltpu.PrefetchScalarGridSpec(
        num_scalar_prefetch=2,
        grid=(nseq, len(pairs)),
        in_specs=[pl.BlockSpec((MLA_HEADS, tq, KCAT), lambda b, p, qt, kt: (0, b * nq + qt[p], 0)),
                  pl.BlockSpec((tk, KCAT), lambda b, p, qt, kt: (b * nk + kt[p], 0)),
                  pl.BlockSpec((MLA_HEADS, MLA_KV_RANK, MLA_HEADS * MLA_V_DIM), lambda b, p, qt, kt: (0, 0, 0))],
        out_specs=pl.BlockSpec((tq, MLA_HEADS * MLA_V_DIM), lambda b, p, qt, kt: (b * nq + qt[p], 0)),
        scratch_shapes=[pltpu.VMEM((rows, 1), F32), pltpu.VMEM((rows, LANE), F32),
                        pltpu.VMEM((rows, MLA_KV_RANK), F32)],
    )
    return pl.pallas_call(
        functools.partial(_attn_prompt_kernel, tq=tq, tk=tk),
        grid_spec=grid_spec,
        out_shape=jax.ShapeDtypeStruct((n, MLA_HEADS * MLA_V_DIM), BF16),
        compiler_params=_cparams(("parallel", "arbitrary")),
        name="attn_prompt",
    )(qi_tab, ki_tab, qcat, kcat, wuv)


def _attn_sample_kernel(pt_ref, q_ref, kn_ref, wuv_ref, ckv_hbm, krt_hbm, o_ref, cbuf, rbuf, sems,
                        *, layer, ppc, nchunk, sd):
    b = pl.program_id(0)
    nb = pl.num_programs(0)
    rows = MLA_HEADS * sd

    def copies(bb, chunk, slot):
        out = []
        for i in range(ppc):
            page = pt_ref[bb, chunk * ppc + i]
            out.append(pltpu.make_async_copy(ckv_hbm.at[layer, page], cbuf.at[slot, i], sems.at[0, slot]))
            out.append(pltpu.make_async_copy(krt_hbm.at[layer, page],
                                             rbuf.at[slot, :, pl.ds(i * PAGE_SIZE, PAGE_SIZE)], sems.at[1, slot]))
        return out

    @pl.when(b == 0)
    def _():
        for cp in copies(0, 0, 0):
            cp.start()

    q = q_ref[...].reshape(rows, KCAT)
    ql = q[:, :MLA_KV_RANK]
    qr = q[:, MLA_KV_RANK:MLA_KV_RANK + MLA_ROPE]

    def accumulate(state, s, values):
        m, l, acc = state
        m_new = jnp.maximum(m, jnp.max(s, axis=-1, keepdims=True))
        alpha = jnp.exp2(m - m_new)
        p = jnp.exp2(s - m_new)
        l = alpha * l + jnp.sum(p, axis=-1, keepdims=True)
        acc = alpha * acc + jnp.dot(p.astype(BF16), values, preferred_element_type=F32)
        return m_new, l, acc

    state = (jnp.full((rows, 1), NEG, F32), jnp.zeros((rows, 1), F32), jnp.zeros((rows, MLA_KV_RANK), F32))
    for c in range(nchunk):
        slot = c % 2
        if c + 1 < nchunk:
            for cp in copies(b, c + 1, 1 - slot):
                cp.start()
        else:
            @pl.when(b + 1 < nb)
            def _():
                for cp in copies(b + 1, 0, 1 - slot):
                    cp.start()
        for cp in copies(b, c, slot):
            cp.wait()
        kc = cbuf[slot].reshape(ppc * PAGE_SIZE, MLA_KV_RANK).astype(BF16)
        krt = rbuf[slot].astype(BF16)
        s = (lax.dot_general(ql, kc, NT, preferred_element_type=F32)
             + jnp.dot(qr, krt, preferred_element_type=F32))
        state = accumulate(state, s, kc)
    kn = kn_ref[...]
    s = lax.dot_general(q, kn, NT, preferred_element_type=F32)
    qpos = lax.broadcasted_iota(jnp.int32, (rows, sd), 0) % sd
    kpos = lax.broadcasted_iota(jnp.int32, (rows, sd), 1)
    s = jnp.where(kpos <= qpos, s, NEG)
    _, l, acc = accumulate(state, s, kn[:, :MLA_KV_RANK])
    o_ref[...] = _latent_to_heads(acc / l, wuv_ref, sd).astype(BF16)


def _attn_sample(qcat, kcat, wuv, cache_ckv, cache_krope_t, page_table, layer):
    nb, n_pages = page_table.shape
    n = kcat.shape[0]
    sd = n // nb
    ppc = max(1, min(16, n_pages // 2))
    nchunk = n_pages // ppc
    assert n_pages % ppc == 0 and nchunk % 2 == 0 and sd % 8 == 0 and sd & (sd - 1) == 0
    grid_spec = pltpu.PrefetchScalarGridSpec(
        num_scalar_prefetch=1,
        grid=(nb,),
        in_specs=[pl.BlockSpec((MLA_HEADS, sd, KCAT), lambda b, pt: (0, b, 0)),
                  pl.BlockSpec((sd, KCAT), lambda b, pt: (b, 0)),
                  pl.BlockSpec((MLA_HEADS, MLA_KV_RANK, MLA_HEADS * MLA_V_DIM), lambda b, pt: (0, 0, 0)),
                  pl.BlockSpec(memory_space=pl.ANY), pl.BlockSpec(memory_space=pl.ANY)],
        out_specs=pl.BlockSpec((sd, MLA_HEADS * MLA_V_DIM), lambda b, pt: (b, 0)),
        scratch_shapes=[pltpu.VMEM((2, ppc, PAGE_SIZE, MLA_KV_RANK), F32),
                        pltpu.VMEM((2, MLA_ROPE, ppc * PAGE_SIZE), F32),
                        pltpu.SemaphoreType.DMA((2, 2))],
    )
    return pl.pallas_call(
        functools.partial(_attn_sample_kernel, layer=layer, ppc=ppc, nchunk=nchunk, sd=sd),
        grid_spec=grid_spec,
        out_shape=jax.ShapeDtypeStruct((n, MLA_HEADS * MLA_V_DIM), BF16),
        compiler_params=_cparams(("arbitrary",)),
        name="attn_sample",
    )(page_table, qcat, kcat, wuv, cache_ckv, cache_krope_t)


def _post_kernel(y_ref, r_ref, k_ref, v_ref, g_ref, ya_ref, yc_ref, x_ref, rk_ref, lg_ref, lb_ref, bd_ref,
                 wo_ref, g1_ref, b1_ref, o_ref):
    y = y_ref[...]
    bd = bd_ref[...]
    inv = 1.0 / RWKV_HEAD
    mu = _dot3(y, bd) * inv
    yc = y - mu
    var = _dot3(yc * yc, bd) * inv
    yn = yc * lax.rsqrt(var + RWKV_GN_EPS) * lg_ref[...] + lb_ref[...]
    bonus = _dot3(r_ref[...] * k_ref[...] * rk_ref[...], bd) * v_ref[...]
    yb = (yn + bonus) * g_ref[...]
    h = (jnp.dot(ya_ref[...], wo_ref[0:256, :], preferred_element_type=F32)
         + _dotb(yb, wo_ref[256:512, :])
         + jnp.dot(yc_ref[...], wo_ref[512:1024, :], preferred_element_type=F32))
    o_ref[...] = _layer_norm(DEEPNORM_ALPHA * x_ref[...] + h, g1_ref[...], b1_ref[...])


def _post(y, r, k, v, g, ya, yc, x, w):
    n = x.shape[0]
    t = _tile(n, 256)
    row = lambda c: pl.BlockSpec((t, c), lambda i: (i, 0))
    vec = _full((1, RWKV_WIDTH))
    vecd = _full((1, D_MODEL))
    return pl.pallas_call(
        _post_kernel,
        grid=(n // t,),
        in_specs=[row(256)] * 6 + [row(512), row(D_MODEL), vec, vec, vec, _full((256, 256)),
                                   _full((D_MODEL, D_MODEL)), vecd, vecd],
        out_specs=row(D_MODEL),
        out_shape=jax.ShapeDtypeStruct((n, D_MODEL), F32),
        compiler_params=_cparams(("parallel",)),
        name="post",
    )(y, r, k, v, g, ya, yc, x, w["rk"], w["lnx_g"], w["lnx_b"], w["bd"], w["wo"], w["ln1_g"], w["ln1_b"])


def _ffn_kernel(x_ref, wg_ref, wu_ref, wd_ref, g2_ref, b2_ref, o_ref, acc_ref):
    f = pl.program_id(1)

    @pl.when(f == 0)
    def _():
        acc_ref[...] = jnp.zeros(acc_ref.shape, F32)

    xb = x_ref[...].astype(BF16)
    g = jnp.dot(xb, wg_ref[...], preferred_element_type=F32)
    u = jnp.dot(xb, wu_ref[...], preferred_element_type=F32)
    hmid = g * _sigmoid(g) * u
    acc_ref[...] += jnp.dot(hmid.astype(BF16), wd_ref[...], preferred_element_type=F32)

    @pl.when(f == pl.num_programs(1) - 1)
    def _():
        o_ref[...] = _layer_norm(DEEPNORM_ALPHA * x_ref[...] + acc_ref[...], g2_ref[...], b2_ref[...])


def _ffn(x, w):
    n = x.shape[0]
    dff = w["ffn_g"].shape[1]
    tm = _tile(n, 1024)
    tf = 256
    vecd = _full((1, D_MODEL))
    return pl.pallas_call(
        _ffn_kernel,
        grid=(n // tm, dff // tf),
        in_specs=[pl.BlockSpec((tm, D_MODEL), lambda i, f: (i, 0)),
                  pl.BlockSpec((D_MODEL, tf), lambda i, f: (0, f)),
                  pl.BlockSpec((D_MODEL, tf), lambda i, f: (0, f)),
                  pl.BlockSpec((tf, D_MODEL), lambda i, f: (f, 0)), vecd, vecd],
        out_specs=pl.BlockSpec((tm, D_MODEL), lambda i, f: (i, 0)),
        out_shape=jax.ShapeDtypeStruct((n, D_MODEL), F32),
        scratch_shapes=[pltpu.VMEM((tm, D_MODEL), F32)],
        compiler_params=_cparams(("parallel", "arbitrary")),
        name="ffn",
    )(x, w["ffn_g"], w["ffn_u"], w["ffn_d"], w["ln2_g"], w["ln2_b"])


def _moe_kernel(x_ref, rt_ref, wg_ref, wu_ref, wd_ref, g2_ref, b2_ref, o_ref, acc_ref, gate_ref):
    e = pl.program_id(1)
    f = pl.program_id(2)
    first = (e == 0) & (f == 0)
    last = (e == pl.num_programs(1) - 1) & (f == pl.num_programs(2) - 1)

    @pl.when(first)
    def _():
        acc_ref[...] = jnp.zeros(acc_ref.shape, F32)
        logits = jnp.dot(x_ref[...], rt_ref[...], preferred_element_type=F32, precision=lax.Precision.HIGHEST)
        lane = lax.broadcasted_iota(jnp.int32, logits.shape, 1)
        logits = jnp.where(lane < N_EXPERTS, logits, NEG)
        m1 = jnp.max(logits, axis=-1, keepdims=True)
        i1 = jnp.min(jnp.where(logits == m1, lane, LANE), axis=-1, keepdims=True)
        rest = jnp.where(lane == i1, NEG, logits)
        m2 = jnp.max(rest, axis=-1, keepdims=True)
        i2 = jnp.min(jnp.where(rest == m2, lane, LANE), axis=-1, keepdims=True)
        e2 = jnp.exp(m2 - m1)
        g1 = 1.0 / (1.0 + e2)
        gate_ref[...] = jnp.where(lane == i1, g1, 0.0) + jnp.where(lane == i2, e2 * g1, 0.0)

    gates = gate_ref[...]
    lane = lax.broadcasted_iota(jnp.int32, gates.shape, 1)
    gcol = jnp.sum(jnp.where(lane == e, gates, 0.0), axis=-1, keepdims=True)
    xb = x_ref[...].astype(BF16)
    g = jnp.dot(xb, wg_ref[0], preferred_element_type=F32)
    u = jnp.dot(xb, wu_ref[0], preferred_element_type=F32)
    hmid = g * _sigmoid(g) * u * gcol
    acc_ref[...] += jnp.dot(hmid.astype(BF16), wd_ref[0], preferred_element_type=F32)

    @pl.when(last)
    def _():
        o_ref[...] = _layer_norm(DEEPNORM_ALPHA * x_ref[...] + acc_ref[...], g2_ref[...], b2_ref[...])


def _moe(x, w):
    n = x.shape[0]
    ne, _, dff = w["moe_g"].shape
    tm = _tile(n, 1024)
    tf = 512
    vecd = _full((1, D_MODEL))
    return pl.pallas_call(
        _moe_kernel,
        grid=(n // tm, ne, dff // tf),
        in_specs=[pl.BlockSpec((tm, D_MODEL), lambda i, e, f: (i, 0)),
                  _full((D_MODEL, LANE)),
                  pl.BlockSpec((1, D_MODEL, tf), lambda i, e, f: (e, 0, f)),
                  pl.BlockSpec((1, D_MODEL, tf), lambda i, e, f: (e, 0, f)),
                  pl.BlockSpec((1, tf, D_MODEL), lambda i, e, f: (e, f, 0)), vecd, vecd],
        out_specs=pl.BlockSpec((tm, D_MODEL), lambda i, e, f: (i, 0)),
        out_shape=jax.ShapeDtypeStruct((n, D_MODEL), F32),
        scratch_shapes=[pltpu.VMEM((tm, D_MODEL), F32), pltpu.VMEM((tm, LANE), F32)],
        compiler_params=_cparams(("parallel", "arbitrary", "arbitrary")),
        name="moe",
    )(x, w["router"], w["moe_g"], w["moe_u"], w["moe_d"], w["ln2_g"], w["ln2_b"])


_RWKV_PERM = np.concatenate([np.arange(0, 256), np.arange(272, 528), np.arange(528, 784),
                             np.arange(256, 272), np.arange(784, 800), np.arange(800, 832)])
_RWKV_INV = np.argsort(_RWKV_PERM)


def _rot_cols(wr):
    half = MLA_ROPE // 2
    return jnp.concatenate([-wr[..., half:], wr[..., :half]], axis=-1)


def _pad_last(x, width):
    return jnp.pad(x, [(0, 0)] * (x.ndim - 1) + [(0, width - x.shape[-1])])


def _prep_layer(l, p):
    w = {}
    w_in = p["w_in"][l]
    rw = w_in[:, 256:256 + RWKV_COLS][:, _RWKV_PERM]
    mla = w_in[:, 256 + RWKV_COLS:]
    wkr = mla[:, MLA_Q_RANK + MLA_KV_RANK:]
    win = jnp.concatenate([w_in[:, :256], mla[:, :MLA_Q_RANK + MLA_KV_RANK], _pad_last(wkr, LANE),
                           _pad_last(_rot_cols(wkr), LANE), rw], axis=1)
    w["win"] = _pad_last(win, PIN_PAD).astype(BF16)
    wuq = p["mla_wuq"][l]
    wq_rope = wuq[:, :, MLA_NOPE:]
    w["wq"] = jnp.concatenate([_pad_last(wuq[:, :, :MLA_NOPE], LANE).reshape(MLA_Q_RANK, -1),
                               _pad_last(wq_rope, LANE).reshape(MLA_Q_RANK, -1),
                               _pad_last(_rot_cols(wq_rope), LANE).reshape(MLA_Q_RANK, -1)], axis=1).astype(BF16)
    wuk = jnp.transpose(p["mla_wuk"][l], (1, 2, 0))
    w["wuk"] = jnp.pad(wuk, ((0, 0), (0, LANE - MLA_NOPE), (0, 0))).astype(BF16)
    wuv = jnp.transpose(p["mla_wuv"][l], (1, 0, 2))
    eye = jnp.eye(MLA_HEADS, dtype=F32)
    w["wuv"] = (wuv[:, :, None, :] * eye[:, None, :, None]).reshape(MLA_HEADS, MLA_KV_RANK, -1).astype(BF16)
    w["qg"] = p["mla_qnorm_g"][l][None]
    w["kvg"] = p["mla_kvnorm_g"][l][None]
    pw = p["pool_w"][l]
    eye4 = jnp.eye(4, dtype=F32)
    w["pool_w"] = (pw[:, :, None, :] * eye4[:, None, :, None]).reshape(POOL_WIDTH, POOL_WIDTH).astype(BF16)
    w["pool_scale"] = p["pool_scale"][l][None]
    w["mu"] = p["rwkv_mu"][l][_RWKV_PERM][None]
    lora = jnp.zeros((64, 768), F32)
    lora = lora.at[0:16, 0:256].set(p["rwkv_w2"][l]).at[16:32, 256:512].set(p["rwkv_a2"][l])
    w["lora"] = lora.at[32:64, 512:768].set(p["rwkv_g2"][l])
    for name, key in (("w0", "rwkv_w0"), ("a0", "rwkv_a0"), ("kkp", "rwkv_kk"), ("ka", "rwkv_ka"),
                      ("lnx_g", "rwkv_lnx_g"), ("lnx_b", "rwkv_lnx_b")):
        w[name] = p[key][l][None]
    w["rk"] = p["rwkv_rk"][l].reshape(1, RWKV_WIDTH)
    w["bd"] = jnp.kron(jnp.eye(RWKV_HEADS, dtype=F32), jnp.ones((RWKV_HEAD, RWKV_HEAD), F32))
    if l > 0:
        w["v0"] = p["vres_v0"][l - 1][None]
        w["v1"] = _pad_last(p["vres_w1"][l - 1], LANE)
        w["v2"] = jnp.pad(p["vres_w2"][l - 1], ((0, LANE - p["vres_w2"].shape[1]), (0, 0)))
    w["wo"] = p["w_out"][l].astype(BF16)
    for name in ("ln1_g", "ln1_b", "ln2_g", "ln2_b"):
        w[name] = p[name][l][None]
    if l % 2 == 0:
        w["ffn_g"] = p["ffn_w_gate"][l // 2].astype(BF16)
        w["ffn_u"] = p["ffn_w_up"][l // 2].astype(BF16)
        w["ffn_d"] = p["ffn_w_down"][l // 2].astype(BF16)
    else:
        w["router"] = _pad_last(p["moe_router"][l // 2], LANE)
        w["moe_g"] = p["moe_w_gate"][l // 2].astype(BF16)
        w["moe_u"] = p["moe_w_up"][l // 2].astype(BF16)
        w["moe_d"] = p["moe_w_down"][l // 2].astype(BF16)
    return w


def _rope_tables(pos):
    half = MLA_ROPE // 2
    freqs = ROPE_BASE ** (-jnp.arange(half, dtype=F32) / half)
    ang = pos.astype(F32)[:, None] * freqs[None, :]
    cos = jnp.cos(ang)
    sin = jnp.sin(ang)
    return (_pad_last(jnp.concatenate([cos, cos], axis=-1), LANE),
            _pad_last(jnp.concatenate([sin, sin], axis=-1), LANE))


def _state_to_kernel(s):
    return jnp.swapaxes(s, -1, -2).reshape(s.shape[0], RWKV_WIDTH, RWKV_HEAD)


def _state_from_kernel(s):
    return jnp.swapaxes(s.reshape(s.shape[0], RWKV_HEADS, RWKV_HEAD, RWKV_HEAD), -1, -2)


def _channel_mix(x, l, w):
    return _ffn(x, w) if l % 2 == 0 else _moe(x, w)


def _run_prompt(x, layers):
    b, s, _ = x.shape
    xf = x.reshape(b * s, D_MODEL)
    cos, sin = _rope_tables(jnp.tile(jnp.arange(s, dtype=jnp.int32), b))
    vfirst = None
    outs = []
    for l, w in enumerate(layers):
        pa, pb, ckv, krope, kcat, qcat = _proj(xf, cos, sin, w)
        ya = _pool(pa, b, w, full_count=False)
        r, lw, k, v, kk, a, g = _rwkv_pre(pb, b, w, vfirst)
        if vfirst is None:
            vfirst = v
        y, sfin = _scan(r, lw, k, v, kk, a, jnp.zeros((b, RWKV_WIDTH, RWKV_HEAD), F32), c=_tile(s, 64))
        yc = _attn_prompt(qcat, kcat, w["wuv"], b)
        x1 = _post(y, r, k, v, g, ya, yc, xf, w)
        xf = _channel_mix(x1, l, w)
        outs.append((ckv.reshape(b, s, -1), krope.reshape(b, s, -1), pa.reshape(b, s, -1)[:, -POOL_BUF:],
                     pb.reshape(b, s, -1)[:, -1][:, _RWKV_INV], _state_from_kernel(sfin)))
    return (xf.reshape(b, s, D_MODEL),) + tuple(jnp.stack([o[i] for o in outs]) for i in range(5))


def _run_sample(x, past_len, cache_ckv, cache_krope, state_pool, state_shift, state_wkv, page_table, layers):
    b, s, _ = x.shape
    xf = x.reshape(b * s, D_MODEL)
    cos, sin = _rope_tables(jnp.tile(past_len + jnp.arange(s, dtype=jnp.int32), b))
    pool_lead = -(POOL_BUF + s) % 8
    shift_lead = 7
    cache_krope_t = jnp.swapaxes(cache_krope, 2, 3)
    vfirst = None
    outs = []
    for l, w in enumerate(layers):
        pa, pb, ckv, krope, kcat, qcat = _proj(xf, cos, sin, w)
        ext = jnp.concatenate([state_pool[l], pa.reshape(b, s, -1)], axis=1)
        ext_p = jnp.pad(ext, ((0, 0), (pool_lead, 0), (0, 0)))
        ya = _pool(ext_p.reshape(-1, POOL_WIDTH), 1, w, full_count=True)
        ya = ya.reshape(b, -1, POOL_WIDTH)[:, -s:].reshape(b * s, POOL_WIDTH)
        pbe = jnp.concatenate([state_shift[l][:, None, _RWKV_PERM], pb.reshape(b, s, -1)], axis=1)
        pbe = jnp.pad(pbe, ((0, 0), (shift_lead, 0), (0, 0)))
        pre = _rwkv_pre(pbe.reshape(-1, RWKV_COLS), 1, w, vfirst)
        if vfirst is None:
            vfirst = pre[3]
        r, lw, k, v, kk, a, g = (t.reshape(b, -1, RWKV_WIDTH)[:, -s:].reshape(b * s, RWKV_WIDTH) for t in pre)
        y, sfin = _scan(r, lw, k, v, kk, a, _state_to_kernel(state_wkv[l]), c=s)
        yc = _attn_sample(qcat, kcat, w["wuv"], cache_ckv, cache_krope_t, page_table, l)
        x1 = _post(y, r, k, v, g, ya, yc, xf, w)
        xf = _channel_mix(x1, l, w)
        outs.append((ckv.reshape(b, s, -1), krope.reshape(b, s, -1), ext[:, -POOL_BUF:],
                     pb.reshape(b, s, -1)[:, -1][:, _RWKV_INV], _state_from_kernel(sfin)))
    return (xf.reshape(b, s, D_MODEL),) + tuple(jnp.stack([o[i] for o in outs]) for i in range(5))


def kernel(x_prompt, x_sample, cache_ckv, cache_krope, state_pool, state_shift, state_wkv, page_table, ln1_g, ln1_b, ln2_g, ln2_b, w_in, pool_w, pool_scale, rwkv_mu, rwkv_w0, rwkv_w2, rwkv_a0, rwkv_a2, rwkv_g2, rwkv_kk, rwkv_ka, rwkv_rk, rwkv_lnx_g, rwkv_lnx_b, vres_v0, vres_w1, vres_w2, mla_qnorm_g, mla_wuq, mla_kvnorm_g, mla_wuk, mla_wuv, w_out, ffn_w_gate, ffn_w_up, ffn_w_down, moe_router, moe_w_gate, moe_w_up, moe_w_down):
    p = dict(ln1_g=ln1_g, ln1_b=ln1_b, ln2_g=ln2_g, ln2_b=ln2_b, w_in=w_in, pool_w=pool_w, pool_scale=pool_scale,
             rwkv_mu=rwkv_mu, rwkv_w0=rwkv_w0, rwkv_w2=rwkv_w2, rwkv_a0=rwkv_a0, rwkv_a2=rwkv_a2, rwkv_g2=rwkv_g2,
             rwkv_kk=rwkv_kk, rwkv_ka=rwkv_ka, rwkv_rk=rwkv_rk, rwkv_lnx_g=rwkv_lnx_g, rwkv_lnx_b=rwkv_lnx_b,
             vres_v0=vres_v0, vres_w1=vres_w1, vres_w2=vres_w2, mla_qnorm_g=mla_qnorm_g, mla_wuq=mla_wuq,
             mla_kvnorm_g=mla_kvnorm_g, mla_wuk=mla_wuk, mla_wuv=mla_wuv, w_out=w_out, ffn_w_gate=ffn_w_gate,
             ffn_w_up=ffn_w_up, ffn_w_down=ffn_w_down, moe_router=moe_router, moe_w_gate=moe_w_gate,
             moe_w_up=moe_w_up, moe_w_down=moe_w_down)
    layers = [_prep_layer(l, p) for l in range(DEPTH)]
    past_len = page_table.shape[1] * PAGE_SIZE
    prompt = _run_prompt(x_prompt, layers)
    sample = _run_sample(x_sample, past_len, cache_ckv, cache_krope, state_pool, state_shift, state_wkv,
                         page_table, layers)
    return (prompt[0], sample[0]) + prompt[1:] + sample[1:]
```

```python
import functools
import math

import numpy as np
import jax
import jax.numpy as jnp
from jax import lax
from jax.experimental import pallas as pl
from jax.experimental.pallas import tpu as pltpu

F32 = jnp.float32
BF16 = jnp.bfloat16

D_MODEL = 1024
POOL_WIDTH = 256
POOL_WINDOWS = (2, 4, 8, 16)
POOL_GDIM = 64
POOL_BUF = 15
RWKV_HEAD = 64
RWKV_WIDTH = 256
RWKV_HEADS = 4
RWKV_COLS = 832
RWKV_GN_EPS = 64e-5
MLA_HEADS = 8
MLA_V_DIM = 64
MLA_NOPE = 64
MLA_ROPE = 32
MLA_Q_RANK = 384
MLA_KV_RANK = 256
MLA_SCALE = (MLA_NOPE + MLA_ROPE) ** -0.5
QK_SCALE = MLA_SCALE * math.log2(math.e)
ROPE_BASE = 10000.0
PAGE_SIZE = 128
N_EXPERTS = 8
DEPTH = 2
DEEPNORM_ALPHA = (2 * DEPTH) ** 0.25
LN_EPS = 1e-5
RMS_EPS = 1e-6

LANE = 128
KCAT = MLA_KV_RANK + LANE
PIN_PAD = 2048
NEG = -1e30
VMEM_LIMIT = 56 * 1024 * 1024

NN = (((1,), (0,)), ((), ()))
NT = (((1,), (1,)), ((), ()))
TN = (((0,), (0,)), ((), ()))


def _tile(n, pref):
    t = min(n, pref)
    while t > 8 and (n % t or t % 8):
        t -= 8
    assert n % t == 0, (n, pref)
    return t


def _cparams(sem):
    return pltpu.CompilerParams(dimension_semantics=sem, vmem_limit_bytes=VMEM_LIMIT)


def _full(shape):
    nd = len(shape)
    return pl.BlockSpec(shape, lambda *_: (0,) * nd)


def _split(x):
    hi = x.astype(BF16)
    lo = (x - hi.astype(F32)).astype(BF16)
    return hi, lo


def _dot3(a, b, dims=NN):
    ah, al = _split(a)
    bh, bl = _split(b)
    d = lambda p, q: lax.dot_general(p, q, dims, preferred_element_type=F32)
    return d(ah, bh) + d(ah, bl) + d(al, bh)


def _dotb(a, b, dims=NN):
    return lax.dot_general(a.astype(BF16), b.astype(BF16), dims, preferred_element_type=F32)


def _sigmoid(x):
    return 1.0 / (1.0 + jnp.exp(-x))


def _layer_norm(x, g, b):
    mu = jnp.mean(x, axis=-1, keepdims=True)
    xc = x - mu
    var = jnp.mean(xc * xc, axis=-1, keepdims=True)
    return xc * lax.rsqrt(var + LN_EPS) * g + b


def _proj_kernel(x_ref, win_ref, cos_ref, sin_ref, qg_ref, kvg_ref, wq_ref, wuk_ref,
                 pa_ref, pb_ref, ckv_ref, kr_ref, kcat_ref, qcat_ref):
    p = _dotb(x_ref[...], win_ref[...])
    pa_ref[...] = p[:, 0:256]
    pb_ref[...] = p[:, 1152:1152 + RWKV_COLS]
    qc = p[:, 256:640]
    kvc = p[:, 640:896]
    cos = cos_ref[...]
    sin = sin_ref[...]
    kr = p[:, 896:1024] * cos + p[:, 1024:1152] * sin
    ckv = kvc * lax.rsqrt(jnp.mean(kvc * kvc, axis=-1, keepdims=True) + RMS_EPS) * kvg_ref[...]
    ckv_ref[...] = ckv
    kr_ref[...] = kr[:, :MLA_ROPE]
    kcat_ref[...] = jnp.concatenate([ckv, kr], axis=-1).astype(BF16)
    qn = qc * lax.rsqrt(jnp.mean(qc * qc, axis=-1, keepdims=True) + RMS_EPS) * qg_ref[...]
    q = _dotb(qn, wq_ref[...])
    for h in range(MLA_HEADS):
        ql = _dotb(q[:, h * LANE:(h + 1) * LANE], wuk_ref[h])
        qr = (q[:, 1024 + h * LANE:1024 + (h + 1) * LANE] * cos
              + q[:, 2048 + h * LANE:2048 + (h + 1) * LANE] * sin)
        qcat_ref[h] = (jnp.concatenate([ql, qr], axis=-1) * QK_SCALE).astype(BF16)


def _proj(x, cos, sin, w):
    n = x.shape[0]
    t = _tile(n, 256)
    row = lambda c: pl.BlockSpec((t, c), lambda i: (i, 0))
    return pl.pallas_call(
        _proj_kernel,
        grid=(n // t,),
        in_specs=[row(D_MODEL), _full((D_MODEL, PIN_PAD)), row(LANE), row(LANE),
                  _full((1, MLA_Q_RANK)), _full((1, MLA_KV_RANK)),
                  _full((MLA_Q_RANK, 3 * MLA_HEADS * LANE)), _full((MLA_HEADS, LANE, MLA_KV_RANK))],
        out_specs=[row(POOL_WIDTH), row(RWKV_COLS), row(MLA_KV_RANK), row(MLA_ROPE), row(KCAT),
                   pl.BlockSpec((MLA_HEADS, t, KCAT), lambda i: (0, i, 0))],
        out_shape=[jax.ShapeDtypeStruct((n, POOL_WIDTH), F32), jax.ShapeDtypeStruct((n, RWKV_COLS), F32),
                   jax.ShapeDtypeStruct((n, MLA_KV_RANK), F32), jax.ShapeDtypeStruct((n, MLA_ROPE), F32),
                   jax.ShapeDtypeStruct((n, KCAT), BF16), jax.ShapeDtypeStruct((MLA_HEADS, n, KCAT), BF16)],
        compiler_params=_cparams(("parallel",)),
        name="proj",
    )(x, w["win"], cos, sin, w["qg"], w["kvg"], w["wq"], w["wuk"])


def _pool_kernel(p_ref, w_ref, sc_ref, o_ref, e_ref, c2_ref, c4_ref, c8_ref, *, t, full_count):
    i = pl.program_id(1)
    n = t + 16

    @pl.when(i == 0)
    def _():
        e_ref[0:16, :] = jnp.zeros((16, POOL_WIDTH), F32)

    @pl.when(i > 0)
    def _():
        e_ref[0:16, :] = e_ref[t:n, :]

    p = p_ref[...]
    e_ref[16:n, :] = p
    c2_ref[1:n, :] = e_ref[1:n, :] + e_ref[0:n - 1, :]
    c4_ref[3:n, :] = c2_ref[3:n, :] + c2_ref[1:n - 2, :]
    c8_ref[7:n, :] = c4_ref[7:n, :] + c4_ref[3:n - 4, :]
    c16 = c8_ref[16:n, :] + c8_ref[8:n - 8, :]
    lane = lax.broadcasted_iota(jnp.int32, (t, POOL_WIDTH), 1)
    win = jnp.where(lane < 64, c2_ref[16:n, :],
                    jnp.where(lane < 128, c4_ref[16:n, :], jnp.where(lane < 192, c8_ref[16:n, :], c16)))
    wlen = jnp.where(lane < 64, 2, jnp.where(lane < 128, 4, jnp.where(lane < 192, 8, 16)))
    if full_count:
        cnt = wlen
    else:
        pos = i * t + lax.broadcasted_iota(jnp.int32, (t, POOL_WIDTH), 0)
        cnt = jnp.minimum(pos + 1, wlen)
    d = win / cnt.astype(F32) - p
    o_ref[...] = (_dotb(d, w_ref[...]) * sc_ref[...]).astype(BF16)


def _pool(pa, nseq, w, full_count):
    n = pa.shape[0]
    s = n // nseq
    t = _tile(s, 512)
    nt = s // t
    buf = pltpu.VMEM((t + 16, POOL_WIDTH), F32)
    return pl.pallas_call(
        functools.partial(_pool_kernel, t=t, full_count=full_count),
        grid=(nseq, nt),
        in_specs=[pl.BlockSpec((t, POOL_WIDTH), lambda b, i: (b * nt + i, 0)),
                  _full((POOL_WIDTH, POOL_WIDTH)), _full((1, POOL_WIDTH))],
        out_specs=pl.BlockSpec((t, POOL_WIDTH), lambda b, i: (b * nt + i, 0)),
        out_shape=jax.ShapeDtypeStruct((n, POOL_WIDTH), BF16),
        scratch_shapes=[buf, buf, buf, buf],
        compiler_params=_cparams(("arbitrary", "arbitrary")),
        name="pool",
    )(pa, w["pool_w"], w["pool_scale"])


def _rwkv_pre_kernel(*refs, t, has_vres):
    if has_vres:
        (pb_ref, mu_ref, wl_ref, w0_ref, a0_ref, kkp_ref, ka_ref, bd_ref, vf_ref, v0_ref, v1_ref, v2_ref,
         r_ref, lw_ref, k_ref, v_ref, kk_ref, a_ref, g_ref, e_ref) = refs
    else:
        (pb_ref, mu_ref, wl_ref, w0_ref, a0_ref, kkp_ref, ka_ref, bd_ref,
         r_ref, lw_ref, k_ref, v_ref, kk_ref, a_ref, g_ref, e_ref) = refs
    i = pl.program_id(1)

    @pl.when(i == 0)
    def _():
        e_ref[0:8, :] = jnp.zeros((8, RWKV_COLS), F32)

    @pl.when(i > 0)
    def _():
        e_ref[0:8, :] = e_ref[t:t + 8, :]

    pb = pb_ref[...]
    e_ref[8:t + 8, :] = pb
    prev = e_ref[7:t + 7, :]
    xs = pb + (prev - pb) * mu_ref[...]
    r = xs[:, 0:256]
    k = xs[:, 256:512]
    v = xs[:, 512:768]
    lo = xs[:, 768:832]
    lane = lax.broadcasted_iota(jnp.int32, lo.shape, 1)
    act = jnp.where(lane < 16, jnp.tanh(lo), jnp.where(lane < 32, lo, _sigmoid(lo)))
    lora = _dot3(act, wl_ref[...])
    z = -(w0_ref[...] + lora[:, 0:256])
    softplus = jnp.maximum(z, 0.0) + jnp.log(1.0 + jnp.exp(-jnp.abs(z)))
    lw_ref[...] = -jnp.exp(-softplus - 0.5)
    if has_vres:
        vf = vf_ref[...]
        mix = _dot3(_dot3(v, v1_ref[...]), v2_ref[...])
        v = v + (vf - v) * _sigmoid(v0_ref[...] + mix)
    a = _sigmoid(a0_ref[...] + lora[:, 256:512])
    kk = k * kkp_ref[...]
    ss = _dot3(kk * kk, bd_ref[...])
    kk = kk / jnp.maximum(jnp.sqrt(ss), 1e-12)
    r_ref[...] = r
    k_ref[...] = k * (1.0 + (a - 1.0) * ka_ref[...])
    v_ref[...] = v
    kk_ref[...] = kk
    a_ref[...] = a
    g_ref[...] = lora[:, 512:768]


def _rwkv_pre(pb, nseq, w, vfirst):
    n = pb.shape[0]
    s = n // nseq
    t = _tile(s, 256)
    nt = s // t
    has_vres = vfirst is not None
    row = lambda c: pl.BlockSpec((t, c), lambda b, i: (b * nt + i, 0))
    vec = _full((1, RWKV_WIDTH))
    in_specs = [row(RWKV_COLS), _full((1, RWKV_COLS)), _full((64, 768)), vec, vec, vec, vec,
                _full((RWKV_WIDTH, RWKV_WIDTH))]
    args = [pb, w["mu"], w["lora"], w["w0"], w["a0"], w["kkp"], w["ka"], w["bd"]]
    if has_vres:
        in_specs += [row(RWKV_WIDTH), vec, _full((RWKV_WIDTH, LANE)), _full((LANE, RWKV_WIDTH))]
        args += [vfirst, w["v0"], w["v1"], w["v2"]]
    return pl.pallas_call(
        functools.partial(_rwkv_pre_kernel, t=t, has_vres=has_vres),
        grid=(nseq, nt),
        in_specs=in_specs,
        out_specs=[row(RWKV_WIDTH)] * 7,
        out_shape=[jax.ShapeDtypeStruct((n, RWKV_WIDTH), F32)] * 7,
        scratch_shapes=[pltpu.VMEM((t + 8, RWKV_COLS), F32)],
        compiler_params=_cparams(("arbitrary", "arbitrary")),
        name="rwkv_pre",
    )(*args)


def _scan_chunk(r, lw, k, v, kk, a, st, c):
    hc = RWKV_HEADS * c
    lane = lax.broadcasted_iota(jnp.int32, (c, RWKV_WIDTH), 1)
    ri = lax.broadcasted_iota(jnp.int32, (c, c), 0)
    ci = lax.broadcasted_iota(jnp.int32, (c, c), 1)
    tril = jnp.where(ri >= ci, 1.0, 0.0).astype(F32)
    cum = jnp.dot(tril, lw, preferred_element_type=F32, precision=lax.Precision.HIGHEST)
    tot = cum[c - 1:c, :]
    e_neg = jnp.exp(-cum)
    at = -kk * jnp.exp(cum - lw)
    bt = kk * a * e_neg
    kt = k * e_neg
    rt = r * jnp.exp(cum)
    e_rem = jnp.exp(tot - cum)
    bh = kk * a * e_rem
    kh = k * e_rem

    def bd(x):
        return jnp.concatenate([jnp.where((lane >= h * RWKV_HEAD) & (lane < (h + 1) * RWKV_HEAD), x, 0.0)
                                for h in range(RWKV_HEADS)], axis=0)

    lst = jnp.concatenate([bd(at), bd(rt)], axis=0)
    rst = jnp.concatenate([bd(bt), bd(kt)], axis=0)
    g = _dot3(lst, rst, NT)
    gr = lax.broadcasted_iota(jnp.int32, (hc, hc), 0) % c
    gc = lax.broadcasted_iota(jnp.int32, (hc, hc), 1) % c
    strict = gc < gr
    incl = gc <= gr
    n_ab = jnp.where(strict, g[0:hc, 0:hc], 0.0)
    a_ak = jnp.where(strict, g[0:hc, hc:2 * hc], 0.0)
    a_rb = jnp.where(incl, g[hc:2 * hc, 0:hc], 0.0)
    a_rk = jnp.where(incl, g[hc:2 * hc, hc:2 * hc], 0.0)
    eye = (lax.broadcasted_iota(jnp.int32, (hc, hc), 0) == lax.broadcasted_iota(jnp.int32, (hc, hc), 1))
    inv = jnp.where(eye, 1.0, 0.0) + n_ab
    npow = n_ab
    for it in range(int(math.log2(c)) - 1):
        dot = _dot3 if it < 2 else _dotb
        npow = dot(npow, npow)
        inv = inv + dot(inv, npow)
    w12 = _dot3(lst, st)
    vr = jnp.concatenate([v[:, h * RWKV_HEAD:(h + 1) * RWKV_HEAD] for h in range(RWKV_HEADS)], axis=0)
    u = _dot3(inv, w12[0:hc] + _dot3(a_ak, vr))
    yr = w12[hc:2 * hc] + _dot3(a_rb, u) + _dot3(a_rk, vr)
    y = jnp.concatenate([yr[h * c:(h + 1) * c, :] for h in range(RWKV_HEADS)], axis=1)
    e256r = lax.broadcasted_iota(jnp.int32, (RWKV_WIDTH, RWKV_WIDTH), 0)
    e256c = lax.broadcasted_iota(jnp.int32, (RWKV_WIDTH, RWKV_WIDTH), 1)
    dec = jnp.where(e256r == e256c, jnp.broadcast_to(jnp.exp(tot), (RWKV_WIDTH, RWKV_WIDTH)), 0.0)
    lhs = jnp.concatenate([dec, bd(bh), bd(kh)], axis=0)
    rhs = jnp.concatenate([st, u, vr], axis=0)
    return y, _dot3(lhs, rhs, TN)


def _scan_kernel(r_ref, lw_ref, k_ref, v_ref, kk_ref, a_ref, s0_ref, y_ref, sf_ref, st_ref, *, c, nch, nb):
    j = pl.program_id(1)

    @pl.when(j == 0)
    def _():
        st_ref[...] = s0_ref[...]

    def body(ch, carry):
        sl = pl.ds(pl.multiple_of(ch * c, c), c)
        for i in range(nb):
            y, st = _scan_chunk(r_ref[i, sl, :], lw_ref[i, sl, :], k_ref[i, sl, :], v_ref[i, sl, :],
                                kk_ref[i, sl, :], a_ref[i, sl, :], st_ref[i], c)
            y_ref[i, sl, :] = y
            st_ref[i] = st
        return carry

    lax.fori_loop(0, nch, body, 0)

    @pl.when(j == pl.num_programs(1) - 1)
    def _():
        sf_ref[...] = st_ref[...]


def _scan(r, lw, k, v, kk, a, s0, c):
    n = r.shape[0]
    nseq = s0.shape[0]
    s = n // nseq
    nb = 2 if s > c else math.gcd(nseq, 8)
    nch = max(1, min(8, s // c))
    while (s // c) % nch:
        nch -= 1
    t = c * nch
    nt = s // t
    row = pl.BlockSpec((nb, t, RWKV_WIDTH), lambda b, i: (b, i, 0))
    st_spec = pl.BlockSpec((nb, RWKV_WIDTH, RWKV_HEAD), lambda b, i: (b, 0, 0))
    seqs = [x.reshape(nseq, s, RWKV_WIDTH) for x in (r, lw, k, v, kk, a)]
    y, sfin = pl.pallas_call(
        functools.partial(_scan_kernel, c=c, nch=nch, nb=nb),
        grid=(nseq // nb, nt),
        in_specs=[row] * 6 + [st_spec],
        out_specs=[row, st_spec],
        out_shape=[jax.ShapeDtypeStruct((nseq, s, RWKV_WIDTH), F32),
                   jax.ShapeDtypeStruct((nseq, RWKV_WIDTH, RWKV_HEAD), F32)],
        scratch_shapes=[pltpu.VMEM((nb, RWKV_WIDTH, RWKV_HEAD), F32)],
        compiler_params=_cparams(("arbitrary", "arbitrary")),
        name="wkv_scan",
    )(*seqs, s0)
    return y.reshape(n, RWKV_WIDTH), sfin


def _latent_to_heads(o, wuv_ref, tq):
    out = None
    for h in range(MLA_HEADS):
        part = _dotb(o[h * tq:(h + 1) * tq, :], wuv_ref[h])
        out = part if out is None else out + part
    return out


CHAIN_ROWS = 1024


def _attn_prompt_kernel(qi_ref, ki_ref, q_ref, k_ref, wuv_ref, o_ref, m_ref, l_ref, acc_ref, *, tq, tk):
    step = pl.program_id(1)
    qi = qi_ref[step]
    ki = ki_ref[step]
    rows = MLA_HEADS * tq

    @pl.when(ki == 0)
    def _():
        m_ref[...] = jnp.full((rows, 1), NEG, F32)
        l_ref[...] = jnp.zeros((rows, LANE), F32)
        acc_ref[...] = jnp.zeros((rows, MLA_KV_RANK), F32)

    last = (qi * tq + tq - 1) // tk
    cheads = max(1, CHAIN_ROWS // tq)
    crows = cheads * tq

    def update(masked):
        k = k_ref[...]
        kv = k[:, :MLA_KV_RANK]
        if masked:
            rel = (lax.broadcasted_iota(jnp.int32, (crows, tk), 1)
                   - lax.broadcasted_iota(jnp.int32, (crows, tk), 0) % tq)
            visible = rel <= qi * tq - ki * tk
        for c in range(MLA_HEADS // cheads):
            r0 = c * crows
            q = q_ref[c * cheads:(c + 1) * cheads].reshape(crows, KCAT)
            s = lax.dot_general(q, k, NT, preferred_element_type=F32)
            if masked:
                s = jnp.where(visible, s, NEG)
            m_old = m_ref[r0:r0 + crows, :]
            m_new = jnp.maximum(m_old, jnp.max(s, axis=-1, keepdims=True))
            alpha = jnp.exp2(m_old - m_new)
            p = jnp.exp2(s - m_new)
            psum = p[:, 0:LANE]
            for t in range(1, tk // LANE):
                psum = psum + p[:, t * LANE:(t + 1) * LANE]
            l_ref[r0:r0 + crows, :] = alpha * l_ref[r0:r0 + crows, :] + psum
            acc_ref[r0:r0 + crows, :] = alpha * acc_ref[r0:r0 + crows, :] + jnp.dot(
                p.astype(BF16), kv, preferred_element_type=F32)
            m_ref[r0:r0 + crows, :] = m_new

    @pl.when(ki < last)
    def _():
        update(False)

    @pl.when(ki == last)
    def _():
        update(True)
        o = acc_ref[...] / jnp.sum(l_ref[...], axis=-1, keepdims=True)
        o_ref[...] = _latent_to_heads(o, wuv_ref, tq).astype(BF16)


def _attn_prompt(qcat, kcat, wuv, nseq):
    n = kcat.shape[0]
    s = n // nseq
    tq = _tile(s, 512)
    tk = _tile(s, 512)
    nq, nk = s // tq, s // tk
    assert tk % LANE == 0 and tk % tq == 0
    pairs = [(qi, ki) for qi in range(nq) for ki in range((qi * tq + tq - 1) // tk + 1)]
    qi_tab = jnp.asarray(np.array([p[0] for p in pairs], np.int32))
    ki_tab = jnp.asarray(np.array([p[1] for p in pairs], np.int32))
    rows = MLA_HEADS * tq
    grid_spec = pltpu.PrefetchScalarGridSpec(
        num_scalar_prefetch=2,
        grid=(nseq, len(pairs)),
        in_specs=[pl.BlockSpec((MLA_HEADS, tq, KCAT), lambda b, p, qt, kt: (0, b * nq + qt[p], 0)),
                  pl.BlockSpec((tk, KCAT), lambda b, p, qt, kt: (b * nk + kt[p], 0)),
                  pl.BlockSpec((MLA_HEADS, MLA_KV_RANK, MLA_HEADS * MLA_V_DIM), lambda b, p, qt, kt: (0, 0, 0))],
        out_specs=pl.BlockSpec((tq, MLA_HEADS * MLA_V_DIM), lambda b, p, qt, kt: (b * nq + qt[p], 0)),
        scratch_shapes=[pltpu.VMEM((rows, 1), F32), pltpu.VMEM((rows, LANE), F32),
                        pltpu.VMEM((rows, MLA_KV_RANK), F32)],
    )
    return pl.pallas_call(
        functools.partial(_attn_prompt_kernel, tq=tq, tk=tk),
        grid_spec=grid_spec,
        out_shape=jax.ShapeDtypeStruct((n, MLA_HEADS * MLA_V_DIM), BF16),
        compiler_params=_cparams(("parallel", "arbitrary")),
        name="attn_prompt",
    )(qi_tab, ki_tab, qcat, kcat, wuv)


def _attn_sample_kernel(pt_ref, q_ref, kn_ref, wuv_ref, ckv_hbm, krt_hbm, o_ref, cbuf, rbuf, sems,
                        *, layer, ppc, nchunk, sd):
    b = pl.program_id(0)
    nb = pl.num_programs(0)
    rows = MLA_HEADS * sd

    def copies(bb, chunk, slot):
        out = []
        for i in range(ppc):
            page = pt_ref[bb, chunk * ppc + i]
            out.append(pltpu.make_async_copy(ckv_hbm.at[layer, page], cbuf.at[slot, i], sems.at[0, slot]))
            out.append(pltpu.make_async_copy(krt_hbm.at[layer, page],
                                             rbuf.at[slot, :, pl.ds(i * PAGE_SIZE, PAGE_SIZE)], sems.at[1, slot]))
        return out

    @pl.when(b == 0)
    def _():
        for cp in copies(0, 0, 0):
            cp.start()

    q = q_ref[...].reshape(rows, KCAT)
    ql = q[:, :MLA_KV_RANK]
    qr = q[:, MLA_KV_RANK:MLA_KV_RANK + MLA_ROPE]

    def accumulate(state, s, values):
        m, l, acc = state
        m_new = jnp.maximum(m, jnp.max(s, axis=-1, keepdims=True))
        alpha = jnp.exp2(m - m_new)
        p = jnp.exp2(s - m_new)
        l = alpha * l + jnp.sum(p, axis=-1, keepdims=True)
        acc = alpha * acc + jnp.dot(p.astype(BF16), values, preferred_element_type=F32)
        return m_new, l, acc

    state = (jnp.full((rows, 1), NEG, F32), jnp.zeros((rows, 1), F32), jnp.zeros((rows, MLA_KV_RANK), F32))
    for c in range(nchunk):
        slot = c % 2
        if c + 1 < nchunk:
            for cp in copies(b, c + 1, 1 - slot):
                cp.start()
        else:
            @pl.when(b + 1 < nb)
            def _():
                for cp in copies(b + 1, 0, 1 - slot):
                    cp.start()
        for cp in copies(b, c, slot):
            cp.wait()
        kc = cbuf[slot].reshape(ppc * PAGE_SIZE, MLA_KV_RANK).astype(BF16)
        krt = rbuf[slot].astype(BF16)
        s = (lax.dot_general(ql, kc, NT, preferred_element_type=F32)
             + jnp.dot(qr, krt, preferred_element_type=F32))
        state = accumulate(state, s, kc)
    kn = kn_ref[...]
    s = lax.dot_general(q, kn, NT, preferred_element_type=F32)
    qpos = lax.broadcasted_iota(jnp.int32, (rows, sd), 0) % sd
    kpos = lax.broadcasted_iota(jnp.int32, (rows, sd), 1)
    s = jnp.where(kpos <= qpos, s, NEG)
    _, l, acc = accumulate(state, s, kn[:, :MLA_KV_RANK])
    o_ref[...] = _latent_to_heads(acc / l, wuv_ref, sd).astype(BF16)


def _attn_sample(qcat, kcat, wuv, cache_ckv, cache_krope_t, page_table, layer):
    nb, n_pages = page_table.shape
    n = kcat.shape[0]
    sd = n // nb
    ppc = max(1, min(16, n_pages // 2))
    nchunk = n_pages // ppc
    assert n_pages % ppc == 0 and nchunk % 2 == 0 and sd % 8 == 0 and sd & (sd - 1) == 0
    grid_spec = pltpu.PrefetchScalarGridSpec(
        num_scalar_prefetch=1,
        grid=(nb,),
        in_specs=[pl.BlockSpec((MLA_HEADS, sd, KCAT), lambda b, pt: (0, b, 0)),
                  pl.BlockSpec((sd, KCAT), lambda b, pt: (b, 0)),
                  pl.BlockSpec((MLA_HEADS, MLA_KV_RANK, MLA_HEADS * MLA_V_DIM), lambda b, pt: (0, 0, 0)),
                  pl.BlockSpec(memory_space=pl.ANY), pl.BlockSpec(memory_space=pl.ANY)],
        out_specs=pl.BlockSpec((sd, MLA_HEADS * MLA_V_DIM), lambda b, pt: (b, 0)),
        scratch_shapes=[pltpu.VMEM((2, ppc, PAGE_SIZE, MLA_KV_RANK), F32),
                        pltpu.VMEM((2, MLA_ROPE, ppc * PAGE_SIZE), F32),
                        pltpu.SemaphoreType.DMA((2, 2))],
    )
    return pl.pallas_call(
        functools.partial(_attn_sample_kernel, layer=layer, ppc=ppc, nchunk=nchunk, sd=sd),
        grid_spec=grid_spec,
        out_shape=jax.ShapeDtypeStruct((n, MLA_HEADS * MLA_V_DIM), BF16),
        compiler_params=_cparams(("arbitrary",)),
        name="attn_sample",
    )(page_table, qcat, kcat, wuv, cache_ckv, cache_krope_t)


def _post_kernel(y_ref, r_ref, k_ref, v_ref, g_ref, ya_ref, yc_ref, x_ref, rk_ref, lg_ref, lb_ref, bd_ref,
                 wo_ref, g1_ref, b1_ref, o_ref):
    y = y_ref[...]
    bd = bd_ref[...]
    inv = 1.0 / RWKV_HEAD
    mu = _dot3(y, bd) * inv
    yc = y - mu
    var = _dot3(yc * yc, bd) * inv
    yn = yc * lax.rsqrt(var + RWKV_GN_EPS) * lg_ref[...] + lb_ref[...]
    bonus = _dot3(r_ref[...] * k_ref[...] * rk_ref[...], bd) * v_ref[...]
    yb = (yn + bonus) * g_ref[...]
    h = (jnp.dot(ya_ref[...], wo_ref[0:256, :], preferred_element_type=F32)
         + _dotb(yb, wo_ref[256:512, :])
         + jnp.dot(yc_ref[...], wo_ref[512:1024, :], preferred_element_type=F32))
    o_ref[...] = _layer_norm(DEEPNORM_ALPHA * x_ref[...] + h, g1_ref[...], b1_ref[...])


def _post(y, r, k, v, g, ya, yc, x, w):
    n = x.shape[0]
    t = _tile(n, 256)
    row = lambda c: pl.BlockSpec((t, c), lambda i: (i, 0))
    vec = _full((1, RWKV_WIDTH))
    vecd = _full((1, D_MODEL))
    return pl.pallas_call(
        _post_kernel,
        grid=(n // t,),
        in_specs=[row(256)] * 6 + [row(512), row(D_MODEL), vec, vec, vec, _full((256, 256)),
                                   _full((D_MODEL, D_MODEL)), vecd, vecd],
        out_specs=row(D_MODEL),
        out_shape=jax.ShapeDtypeStruct((n, D_MODEL), F32),
        compiler_params=_cparams(("parallel",)),
        name="post",
    )(y, r, k, v, g, ya, yc, x, w["rk"], w["lnx_g"], w["lnx_b"], w["bd"], w["wo"], w["ln1_g"], w["ln1_b"])


def _ffn_kernel(x_ref, wg_ref, wu_ref, wd_ref, g2_ref, b2_ref, o_ref, acc_ref):
    f = pl.program_id(1)

    @pl.when(f == 0)
    def _():
        acc_ref[...] = jnp.zeros(acc_ref.shape, F32)

    xb = x_ref[...].astype(BF16)
    g = jnp.dot(xb, wg_ref[...], preferred_element_type=F32)
    u = jnp.dot(xb, wu_ref[...], preferred_element_type=F32)
    hmid = g * _sigmoid(g) * u
    acc_ref[...] += jnp.dot(hmid.astype(BF16), wd_ref[...], preferred_element_type=F32)

    @pl.when(f == pl.num_programs(1) - 1)
    def _():
        o_ref[...] = _layer_norm(DEEPNORM_ALPHA * x_ref[...] + acc_ref[...], g2_ref[...], b2_ref[...])


def _ffn(x, w):
    n = x.shape[0]
    dff = w["ffn_g"].shape[1]
    tm = _tile(n, 1024)
    tf = 256
    vecd = _full((1, D_MODEL))
    return pl.pallas_call(
        _ffn_kernel,
        grid=(n // tm, dff // tf),
        in_specs=[pl.BlockSpec((tm, D_MODEL), lambda i, f: (i, 0)),
                  pl.BlockSpec((D_MODEL, tf), lambda i, f: (0, f)),
                  pl.BlockSpec((D_MODEL, tf), lambda i, f: (0, f)),
                  pl.BlockSpec((tf, D_MODEL), lambda i, f: (f, 0)), vecd, vecd],
        out_specs=pl.BlockSpec((tm, D_MODEL), lambda i, f: (i, 0)),
        out_shape=jax.ShapeDtypeStruct((n, D_MODEL), F32),
        scratch_shapes=[pltpu.VMEM((tm, D_MODEL), F32)],
        compiler_params=_cparams(("parallel", "arbitrary")),
        name="ffn",
    )(x, w["ffn_g"], w["ffn_u"], w["ffn_d"], w["ln2_g"], w["ln2_b"])


MOE_TILE = 512
IDX_BLOCK = 1024
TOP_K = 2


def _router_kernel(x_ref, rt_ref, tri_ref, meta_ref, cnt_ref, run_ref):
    i = pl.program_id(0)

    @pl.when(i == 0)
    def _():
        run_ref[...] = jnp.zeros(run_ref.shape, F32)

    logits = jnp.dot(x_ref[...], rt_ref[...], preferred_element_type=F32, precision=lax.Precision.HIGHEST)
    lane = lax.broadcasted_iota(jnp.int32, logits.shape, 1).astype(F32)
    logits = jnp.where(lane < N_EXPERTS, logits, NEG)
    m1 = jnp.max(logits, axis=-1, keepdims=True)
    i1 = jnp.min(jnp.where(logits == m1, lane, float(LANE)), axis=-1, keepdims=True)
    rest = jnp.where(lane == i1, NEG, logits)
    m2 = jnp.max(rest, axis=-1, keepdims=True)
    i2 = jnp.min(jnp.where(rest == m2, lane, float(LANE)), axis=-1, keepdims=True)
    e2 = jnp.exp(m2 - m1)
    g1 = 1.0 / (1.0 + e2)
    g2 = e2 * g1
    onehot = jnp.where(lane == i1, 1.0, jnp.where(lane == i2, 1.0, 0.0))
    before = jnp.dot(tri_ref[...], onehot.astype(BF16), preferred_element_type=F32) + run_ref[...]
    r1 = jnp.sum(jnp.where(lane == i1, before, 0.0), axis=-1, keepdims=True)
    r2 = jnp.sum(jnp.where(lane == i2, before, 0.0), axis=-1, keepdims=True)
    run_ref[...] += jnp.sum(onehot, axis=0, keepdims=True)
    meta = jnp.zeros(logits.shape, F32)
    for col, val in enumerate((i1, i2, g1, g2, r1, r2)):
        meta = jnp.where(lane == col, val, meta)
    meta_ref[...] = meta
    cnt_ref[...] = run_ref[...]


def _gather_rows_kernel(src_ref, x_hbm, o_ref, buf, sem):
    def copy(r):
        tok = src_ref[r // LANE, r % LANE]
        return pltpu.make_async_copy(x_hbm.at[pl.ds(tok, 1)], buf.at[pl.ds(r, 1)], sem)

    def start(r, c):
        copy(r).start()
        return c

    def wait(r, c):
        copy(r).wait()
        return c

    lax.fori_loop(0, IDX_BLOCK, start, 0, unroll=8)
    lax.fori_loop(0, IDX_BLOCK, wait, 0, unroll=8)
    o_ref[...] = buf[...].astype(BF16)


def _experts_kernel(te_ref, nv_ref, x_ref, wg_ref, wu_ref, wd_ref, o_ref, acc_ref):
    i = pl.program_id(0)
    f = pl.program_id(1)
    last = pl.num_programs(1) - 1

    @pl.when(i < nv_ref[0])
    def _():
        @pl.when(f == 0)
        def _():
            acc_ref[...] = jnp.zeros(acc_ref.shape, F32)

        xb = x_ref[...]
        g = jnp.dot(xb, wg_ref[0], preferred_element_type=F32)
        u = jnp.dot(xb, wu_ref[0], preferred_element_type=F32)
        hmid = g * _sigmoid(g) * u
        acc_ref[...] += jnp.dot(hmid.astype(BF16), wd_ref[0], preferred_element_type=F32)

        @pl.when(f == last)
        def _():
            o_ref[...] = acc_ref[...]

    @pl.when((i >= nv_ref[0]) & (f == last))
    def _():
        o_ref[...] = jnp.zeros(o_ref.shape, F32)


def _combine_kernel(d1_ref, d2_ref, meta_ref, x_ref, g2_ref, b2_ref, ys_hbm, o_ref, buf1, buf2, sems):
    def copies(r):
        a = d1_ref[r // LANE, r % LANE]
        b = d2_ref[r // LANE, r % LANE]
        return (pltpu.make_async_copy(ys_hbm.at[pl.ds(a, 1)], buf1.at[pl.ds(r, 1)], sems.at[0]),
                pltpu.make_async_copy(ys_hbm.at[pl.ds(b, 1)], buf2.at[pl.ds(r, 1)], sems.at[1]))

    def start(r, c):
        for cp in copies(r):
            cp.start()
        return c

    def wait(r, c):
        for cp in copies(r):
            cp.wait()
        return c

    lax.fori_loop(0, IDX_BLOCK, start, 0, unroll=8)
    lax.fori_loop(0, IDX_BLOCK, wait, 0, unroll=8)
    meta = meta_ref[...]
    y = meta[:, 2:3] * buf1[...] + meta[:, 3:4] * buf2[...]
    o_ref[...] = _layer_norm(DEEPNORM_ALPHA * x_ref[...] + y, g2_ref[...], b2_ref[...])


def _moe(x, w):
    n = x.shape[0]
    ne, _, dff = w["moe_g"].shape
    assert n % IDX_BLOCK == 0 and ne == N_EXPERTS
    tr = IDX_BLOCK
    meta, cnt = pl.pallas_call(
        _router_kernel,
        grid=(n // tr,),
        in_specs=[pl.BlockSpec((tr, D_MODEL), lambda i: (i, 0)), _full((D_MODEL, LANE)), _full((tr, tr))],
        out_specs=[pl.BlockSpec((tr, LANE), lambda i: (i, 0)), _full((1, LANE))],
        out_shape=[jax.ShapeDtypeStruct((n, LANE), F32), jax.ShapeDtypeStruct((1, LANE), F32)],
        scratch_shapes=[pltpu.VMEM((1, LANE), F32)],
        compiler_params=_cparams(("arbitrary",)),
        name="moe_router",
    )(x, w["router"], w["tri"])

    i1 = meta[:, 0].astype(jnp.int32)
    i2 = meta[:, 1].astype(jnp.int32)
    counts = cnt[0, :ne].astype(jnp.int32)
    padded = (counts + MOE_TILE - 1) // MOE_TILE * MOE_TILE
    ends = jnp.cumsum(padded)
    starts = ends - padded
    d1 = starts[i1] + meta[:, 4].astype(jnp.int32)
    d2 = starts[i2] + meta[:, 5].astype(jnp.int32)
    a_pad = TOP_K * n + ne * MOE_TILE
    tok = jnp.arange(n, dtype=jnp.int32)
    src = jnp.zeros((a_pad,), jnp.int32).at[d1].set(tok).at[d2].set(tok)
    n_tiles = a_pad // MOE_TILE
    tile_expert = jnp.minimum(jnp.searchsorted(ends, jnp.arange(n_tiles, dtype=jnp.int32) * MOE_TILE,
                                               side="right"), ne - 1).astype(jnp.int32)
    n_valid = (ends[-1] // MOE_TILE).astype(jnp.int32).reshape(1)

    idx_spec = pl.BlockSpec((IDX_BLOCK // LANE, LANE), lambda i: (i, 0), memory_space=pltpu.SMEM)
    xs = pl.pallas_call(
        _gather_rows_kernel,
        grid=(a_pad // IDX_BLOCK,),
        in_specs=[idx_spec, pl.BlockSpec(memory_space=pl.ANY)],
        out_specs=pl.BlockSpec((IDX_BLOCK, D_MODEL), lambda i: (i, 0)),
        out_shape=jax.ShapeDtypeStruct((a_pad, D_MODEL), BF16),
        scratch_shapes=[pltpu.VMEM((IDX_BLOCK, D_MODEL), F32), pltpu.SemaphoreType.DMA(())],
        compiler_params=_cparams(("arbitrary",)),
        name="moe_gather",
    )(src.reshape(a_pad // LANE, LANE), x)

    tf = 896 if dff % 896 == 0 else 512
    ys = pl.pallas_call(
        _experts_kernel,
        grid_spec=pltpu.PrefetchScalarGridSpec(
            num_scalar_prefetch=2,
            grid=(n_tiles, dff // tf),
            in_specs=[pl.BlockSpec((MOE_TILE, D_MODEL), lambda i, f, te, nv: (i, 0)),
                      pl.BlockSpec((1, D_MODEL, tf), lambda i, f, te, nv: (te[i], 0, f)),
                      pl.BlockSpec((1, D_MODEL, tf), lambda i, f, te, nv: (te[i], 0, f)),
                      pl.BlockSpec((1, tf, D_MODEL), lambda i, f, te, nv: (te[i], f, 0))],
            out_specs=pl.BlockSpec((MOE_TILE, D_MODEL), lambda i, f, te, nv: (i, 0)),
            scratch_shapes=[pltpu.VMEM((MOE_TILE, D_MODEL), F32)],
        ),
        out_shape=jax.ShapeDtypeStruct((a_pad, D_MODEL), F32),
        compiler_params=_cparams(("arbitrary", "arbitrary")),
        name="moe_experts",
    )(tile_expert, n_valid, xs, w["moe_g"], w["moe_u"], w["moe_d"])

    vecd = _full((1, D_MODEL))
    row = lambda c: pl.BlockSpec((IDX_BLOCK, c), lambda i: (i, 0))
    return pl.pallas_call(
        _combine_kernel,
        grid=(n // IDX_BLOCK,),
        in_specs=[idx_spec, idx_spec, row(LANE), row(D_MODEL), vecd, vecd, pl.BlockSpec(memory_space=pl.ANY)],
        out_specs=row(D_MODEL),
        out_shape=jax.ShapeDtypeStruct((n, D_MODEL), F32),
        scratch_shapes=[pltpu.VMEM((IDX_BLOCK, D_MODEL), F32), pltpu.VMEM((IDX_BLOCK, D_MODEL), F32),
                        pltpu.SemaphoreType.DMA((2,))],
        compiler_params=_cparams(("arbitrary",)),
        name="moe_combine",
    )(d1.reshape(n // LANE, LANE), d2.reshape(n // LANE, LANE), meta, x, w["ln2_g"], w["ln2_b"], ys)


_RWKV_PERM = np.concatenate([np.arange(0, 256), np.arange(272, 528), np.arange(528, 784),
                             np.arange(256, 272), np.arange(784, 800), np.arange(800, 832)])
_RWKV_INV = np.argsort(_RWKV_PERM)


def _rot_cols(wr):
    half = MLA_ROPE // 2
    return jnp.concatenate([-wr[..., half:], wr[..., :half]], axis=-1)


def _pad_last(x, width):
    return jnp.pad(x, [(0, 0)] * (x.ndim - 1) + [(0, width - x.shape[-1])])


def _prep_layer(l, p):
    w = {}
    w_in = p["w_in"][l]
    rw = w_in[:, 256:256 + RWKV_COLS][:, _RWKV_PERM]
    mla = w_in[:, 256 + RWKV_COLS:]
    wkr = mla[:, MLA_Q_RANK + MLA_KV_RANK:]
    win = jnp.concatenate([w_in[:, :256], mla[:, :MLA_Q_RANK + MLA_KV_RANK], _pad_last(wkr, LANE),
                           _pad_last(_rot_cols(wkr), LANE), rw], axis=1)
    w["win"] = _pad_last(win, PIN_PAD).astype(BF16)
    wuq = p["mla_wuq"][l]
    wq_rope = wuq[:, :, MLA_NOPE:]
    w["wq"] = jnp.concatenate([_pad_last(wuq[:, :, :MLA_NOPE], LANE).reshape(MLA_Q_RANK, -1),
                               _pad_last(wq_rope, LANE).reshape(MLA_Q_RANK, -1),
                               _pad_last(_rot_cols(wq_rope), LANE).reshape(MLA_Q_RANK, -1)], axis=1).astype(BF16)
    wuk = jnp.transpose(p["mla_wuk"][l], (1, 2, 0))
    w["wuk"] = jnp.pad(wuk, ((0, 0), (0, LANE - MLA_NOPE), (0, 0))).astype(BF16)
    wuv = jnp.transpose(p["mla_wuv"][l], (1, 0, 2))
    eye = jnp.eye(MLA_HEADS, dtype=F32)
    w["wuv"] = (wuv[:, :, None, :] * eye[:, None, :, None]).reshape(MLA_HEADS, MLA_KV_RANK, -1).astype(BF16)
    w["qg"] = p["mla_qnorm_g"][l][None]
    w["kvg"] = p["mla_kvnorm_g"][l][None]
    pw = p["pool_w"][l]
    eye4 = jnp.eye(4, dtype=F32)
    w["pool_w"] = (pw[:, :, None, :] * eye4[:, None, :, None]).reshape(POOL_WIDTH, POOL_WIDTH).astype(BF16)
    w["pool_scale"] = p["pool_scale"][l][None]
    w["mu"] = p["rwkv_mu"][l][_RWKV_PERM][None]
    lora = jnp.zeros((64, 768), F32)
    lora = lora.at[0:16, 0:256].set(p["rwkv_w2"][l]).at[16:32, 256:512].set(p["rwkv_a2"][l])
    w["lora"] = lora.at[32:64, 512:768].set(p["rwkv_g2"][l])
    for name, key in (("w0", "rwkv_w0"), ("a0", "rwkv_a0"), ("kkp", "rwkv_kk"), ("ka", "rwkv_ka"),
                      ("lnx_g", "rwkv_lnx_g"), ("lnx_b", "rwkv_lnx_b")):
        w[name] = p[key][l][None]
    w["rk"] = p["rwkv_rk"][l].reshape(1, RWKV_WIDTH)
    w["bd"] = jnp.kron(jnp.eye(RWKV_HEADS, dtype=F32), jnp.ones((RWKV_HEAD, RWKV_HEAD), F32))
    if l > 0:
        w["v0"] = p["vres_v0"][l - 1][None]
        w["v1"] = _pad_last(p["vres_w1"][l - 1], LANE)
        w["v2"] = jnp.pad(p["vres_w2"][l - 1], ((0, LANE - p["vres_w2"].shape[1]), (0, 0)))
    w["wo"] = p["w_out"][l].astype(BF16)
    for name in ("ln1_g", "ln1_b", "ln2_g", "ln2_b"):
        w[name] = p[name][l][None]
    if l % 2 == 0:
        w["ffn_g"] = p["ffn_w_gate"][l // 2].astype(BF16)
        w["ffn_u"] = p["ffn_w_up"][l // 2].astype(BF16)
        w["ffn_d"] = p["ffn_w_down"][l // 2].astype(BF16)
    else:
        w["router"] = _pad_last(p["moe_router"][l // 2], LANE)
        tri = np.tril(np.ones((IDX_BLOCK, IDX_BLOCK), np.float32), -1)
        w["tri"] = jnp.asarray(tri, BF16)
        w["moe_g"] = p["moe_w_gate"][l // 2].astype(BF16)
        w["moe_u"] = p["moe_w_up"][l // 2].astype(BF16)
        w["moe_d"] = p["moe_w_down"][l // 2].astype(BF16)
    return w


def _rope_tables(pos):
    half = MLA_ROPE // 2
    freqs = ROPE_BASE ** (-jnp.arange(half, dtype=F32) / half)
    ang = pos.astype(F32)[:, None] * freqs[None, :]
    cos = jnp.cos(ang)
    sin = jnp.sin(ang)
    return (_pad_last(jnp.concatenate([cos, cos], axis=-1), LANE),
            _pad_last(jnp.concatenate([sin, sin], axis=-1), LANE))


def _state_to_kernel(s):
    return jnp.swapaxes(s, -1, -2).reshape(s.shape[0], RWKV_WIDTH, RWKV_HEAD)


def _state_from_kernel(s):
    return jnp.swapaxes(s.reshape(s.shape[0], RWKV_HEADS, RWKV_HEAD, RWKV_HEAD), -1, -2)


def _channel_mix(x, l, w):
    return _ffn(x, w) if l % 2 == 0 else _moe(x, w)


def _run_prompt(x, layers):
    b, s, _ = x.shape
    xf = x.reshape(b * s, D_MODEL)
    cos, sin = _rope_tables(jnp.tile(jnp.arange(s, dtype=jnp.int32), b))
    vfirst = None
    outs = []
    for l, w in enumerate(layers):
        pa, pb, ckv, krope, kcat, qcat = _proj(xf, cos, sin, w)
        ya = _pool(pa, b, w, full_count=False)
        r, lw, k, v, kk, a, g = _rwkv_pre(pb, b, w, vfirst)
        if vfirst is None:
            vfirst = v
        y, sfin = _scan(r, lw, k, v, kk, a, jnp.zeros((b, RWKV_WIDTH, RWKV_HEAD), F32), c=_tile(s, 64))
        yc = _attn_prompt(qcat, kcat, w["wuv"], b)
        x1 = _post(y, r, k, v, g, ya, yc, xf, w)
        xf = _channel_mix(x1, l, w)
        outs.append((ckv.reshape(b, s, -1), krope.reshape(b, s, -1), pa.reshape(b, s, -1)[:, -POOL_BUF:],
                     pb.reshape(b, s, -1)[:, -1][:, _RWKV_INV], _state_from_kernel(sfin)))
    return (xf.reshape(b, s, D_MODEL),) + tuple(jnp.stack([o[i] for o in outs]) for i in range(5))


def _run_sample(x, past_len, cache_ckv, cache_krope, state_pool, state_shift, state_wkv, page_table, layers):
    b, s, _ = x.shape
    xf = x.reshape(b * s, D_MODEL)
    cos, sin = _rope_tables(jnp.tile(past_len + jnp.arange(s, dtype=jnp.int32), b))
    pool_lead = -(POOL_BUF + s) % 8
    shift_lead = 7
    cache_krope_t = jnp.swapaxes(cache_krope, 2, 3)
    vfirst = None
    outs = []
    for l, w in enumerate(layers):
        pa, pb, ckv, krope, kcat, qcat = _proj(xf, cos, sin, w)
        ext = jnp.concatenate([state_pool[l], pa.reshape(b, s, -1)], axis=1)
        ext_p = jnp.pad(ext, ((0, 0), (pool_lead, 0), (0, 0)))
        ya = _pool(ext_p.reshape(-1, POOL_WIDTH), 1, w, full_count=True)
        ya = ya.reshape(b, -1, POOL_WIDTH)[:, -s:].reshape(b * s, POOL_WIDTH)
        pbe = jnp.concatenate([state_shift[l][:, None, _RWKV_PERM], pb.reshape(b, s, -1)], axis=1)
        pbe = jnp.pad(pbe, ((0, 0), (shift_lead, 0), (0, 0)))
        pre = _rwkv_pre(pbe.reshape(-1, RWKV_COLS), 1, w, vfirst)
        if vfirst is None:
            vfirst = pre[3]
        r, lw, k, v, kk, a, g = (t.reshape(b, -1, RWKV_WIDTH)[:, -s:].reshape(b * s, RWKV_WIDTH) for t in pre)
        y, sfin = _scan(r, lw, k, v, kk, a, _state_to_kernel(state_wkv[l]), c=s)
        yc = _attn_sample(qcat, kcat, w["wuv"], cache_ckv, cache_krope_t, page_table, l)
        x1 = _post(y, r, k, v, g, ya, yc, xf, w)
        xf = _channel_mix(x1, l, w)
        outs.append((ckv.reshape(b, s, -1), krope.reshape(b, s, -1), ext[:, -POOL_BUF:],
                     pb.reshape(b, s, -1)[:, -1][:, _RWKV_INV], _state_from_kernel(sfin)))
    return (xf.reshape(b, s, D_MODEL),) + tuple(jnp.stack([o[i] for o in outs]) for i in range(5))


def kernel(x_prompt, x_sample, cache_ckv, cache_krope, state_pool, state_shift, state_wkv, page_table, ln1_g, ln1_b, ln2_g, ln2_b, w_in, pool_w, pool_scale, rwkv_mu, rwkv_w0, rwkv_w2, rwkv_a0, rwkv_a2, rwkv_g2, rwkv_kk, rwkv_ka, rwkv_rk, rwkv_lnx_g, rwkv_lnx_b, vres_v0, vres_w1, vres_w2, mla_qnorm_g, mla_wuq, mla_kvnorm_g, mla_wuk, mla_wuv, w_out, ffn_w_gate, ffn_w_up, ffn_w_down, moe_router, moe_w_gate, moe_w_up, moe_w_down):
    p = dict(ln1_g=ln1_g, ln1_b=ln1_b, ln2_g=ln2_g, ln2_b=ln2_b, w_in=w_in, pool_w=pool_w, pool_scale=pool_scale,
             rwkv_mu=rwkv_mu, rwkv_w0=rwkv_w0, rwkv_w2=rwkv_w2, rwkv_a0=rwkv_a0, rwkv_a2=rwkv_a2, rwkv_g2=rwkv_g2,
             rwkv_kk=rwkv_kk, rwkv_ka=rwkv_ka, rwkv_rk=rwkv_rk, rwkv_lnx_g=rwkv_lnx_g, rwkv_lnx_b=rwkv_lnx_b,
             vres_v0=vres_v0, vres_w1=vres_w1, vres_w2=vres_w2, mla_qnorm_g=mla_qnorm_g, mla_wuq=mla_wuq,
             mla_kvnorm_g=mla_kvnorm_g, mla_wuk=mla_wuk, mla_wuv=mla_wuv, w_out=w_out, ffn_w_gate=ffn_w_gate,
             ffn_w_up=ffn_w_up, ffn_w_down=ffn_w_down, moe_router=moe_router, moe_w_gate=moe_w_gate,
             moe_w_up=moe_w_up, moe_w_down=moe_w_down)
    layers = [_prep_layer(l, p) for l in range(DEPTH)]
    past_len = page_table.shape[1] * PAGE_SIZE
    prompt = _run_prompt(x_prompt, layers)
    sample = _run_sample(x_sample, past_len, cache_ckv, cache_krope, state_pool, state_shift, state_wkv,
                         page_table, layers)
    return (prompt[0], sample[0]) + prompt[1:] + sample[1:]
```

```python
import functools
import math

import numpy as np
import jax
import jax.numpy as jnp
from jax import lax
from jax.experimental import pallas as pl
from jax.experimental.pallas import tpu as pltpu

F32 = jnp.float32
BF16 = jnp.bfloat16

D_MODEL = 1024
POOL_WIDTH = 256
POOL_WINDOWS = (2, 4, 8, 16)
POOL_GDIM = 64
POOL_BUF = 15
RWKV_HEAD = 64
RWKV_WIDTH = 256
RWKV_HEADS = 4
RWKV_COLS = 832
RWKV_GN_EPS = 64e-5
MLA_HEADS = 8
MLA_V_DIM = 64
MLA_NOPE = 64
MLA_ROPE = 32
MLA_Q_RANK = 384
MLA_KV_RANK = 256
MLA_SCALE = (MLA_NOPE + MLA_ROPE) ** -0.5
QK_SCALE = MLA_SCALE * math.log2(math.e)
ROPE_BASE = 10000.0
PAGE_SIZE = 128
N_EXPERTS = 8
DEPTH = 2
DEEPNORM_ALPHA = (2 * DEPTH) ** 0.25
LN_EPS = 1e-5
RMS_EPS = 1e-6

LANE = 128
KCAT = MLA_KV_RANK + LANE
PIN_PAD = 2048
NEG = -1e30
VMEM_LIMIT = 56 * 1024 * 1024

NN = (((1,), (0,)), ((), ()))
NT = (((1,), (1,)), ((), ()))
TN = (((0,), (0,)), ((), ()))


def _tile(n, pref):
    t = min(n, pref)
    while t > 8 and (n % t or t % 8):
        t -= 8
    assert n % t == 0, (n, pref)
    return t


def _cparams(sem):
    return pltpu.CompilerParams(dimension_semantics=sem, vmem_limit_bytes=VMEM_LIMIT)


def _full(shape):
    nd = len(shape)
    return pl.BlockSpec(shape, lambda *_: (0,) * nd)


def _split(x):
    hi = x.astype(BF16)
    lo = (x - hi.astype(F32)).astype(BF16)
    return hi, lo


def _dot3(a, b, dims=NN):
    ah, al = _split(a)
    bh, bl = _split(b)
    d = lambda p, q: lax.dot_general(p, q, dims, preferred_element_type=F32)
    return d(ah, bh) + d(ah, bl) + d(al, bh)


def _dotb(a, b, dims=NN):
    return lax.dot_general(a.astype(BF16), b.astype(BF16), dims, preferred_element_type=F32)


def _sigmoid(x):
    return 1.0 / (1.0 + jnp.exp(-x))


def _layer_norm(x, g, b):
    mu = jnp.mean(x, axis=-1, keepdims=True)
    xc = x - mu
    var = jnp.mean(xc * xc, axis=-1, keepdims=True)
    return xc * lax.rsqrt(var + LN_EPS) * g + b


def _proj_kernel(x_ref, win_ref, cos_ref, sin_ref, qg_ref, kvg_ref, wq_ref, wuk_ref,
                 pa_ref, pb_ref, ckv_ref, kr_ref, kcat_ref, qcat_ref):
    p = _dotb(x_ref[...], win_ref[...])
    pa_ref[...] = p[:, 0:256]
    pb_ref[...] = p[:, 1152:1152 + RWKV_COLS]
    qc = p[:, 256:640]
    kvc = p[:, 640:896]
    cos = cos_ref[...]
    sin = sin_ref[...]
    kr = p[:, 896:1024] * cos + p[:, 1024:1152] * sin
    ckv = kvc * lax.rsqrt(jnp.mean(kvc * kvc, axis=-1, keepdims=True) + RMS_EPS) * kvg_ref[...]
    ckv_ref[...] = ckv
    kr_ref[...] = kr[:, :MLA_ROPE]
    kcat_ref[...] = jnp.concatenate([ckv, kr], axis=-1).astype(BF16)
    qn = qc * lax.rsqrt(jnp.mean(qc * qc, axis=-1, keepdims=True) + RMS_EPS) * qg_ref[...]
    q = _dotb(qn, wq_ref[...])
    for h in range(MLA_HEADS):
        ql = _dotb(q[:, h * LANE:(h + 1) * LANE], wuk_ref[h])
        qr = (q[:, 1024 + h * LANE:1024 + (h + 1) * LANE] * cos
              + q[:, 2048 + h * LANE:2048 + (h + 1) * LANE] * sin)
        qcat_ref[h] = (jnp.concatenate([ql, qr], axis=-1) * QK_SCALE).astype(BF16)


def _proj(x, cos, sin, w):
    n = x.shape[0]
    t = _tile(n, 256)
    row = lambda c: pl.BlockSpec((t, c), lambda i: (i, 0))
    return pl.pallas_call(
        _proj_kernel,
        grid=(n // t,),
        in_specs=[row(D_MODEL), _full((D_MODEL, PIN_PAD)), row(LANE), row(LANE),
                  _full((1, MLA_Q_RANK)), _full((1, MLA_KV_RANK)),
                  _full((MLA_Q_RANK, 3 * MLA_HEADS * LANE)), _full((MLA_HEADS, LANE, MLA_KV_RANK))],
        out_specs=[row(POOL_WIDTH), row(RWKV_COLS), row(MLA_KV_RANK), row(MLA_ROPE), row(KCAT),
                   pl.BlockSpec((MLA_HEADS, t, KCAT), lambda i: (0, i, 0))],
        out_shape=[jax.ShapeDtypeStruct((n, POOL_WIDTH), F32), jax.ShapeDtypeStruct((n, RWKV_COLS), F32),
                   jax.ShapeDtypeStruct((n, MLA_KV_RANK), F32), jax.ShapeDtypeStruct((n, MLA_ROPE), F32),
                   jax.ShapeDtypeStruct((n, KCAT), BF16), jax.ShapeDtypeStruct((MLA_HEADS, n, KCAT), BF16)],
        compiler_params=_cparams(("parallel",)),
        name="proj",
    )(x, w["win"], cos, sin, w["qg"], w["kvg"], w["wq"], w["wuk"])


def _pool_kernel(p_ref, w_ref, sc_ref, o_ref, e_ref, c2_ref, c4_ref, c8_ref, *, t, full_count):
    i = pl.program_id(1)
    n = t + 16

    @pl.when(i == 0)
    def _():
        e_ref[0:16, :] = jnp.zeros((16, POOL_WIDTH), F32)

    @pl.when(i > 0)
    def _():
        e_ref[0:16, :] = e_ref[t:n, :]

    p = p_ref[...]
    e_ref[16:n, :] = p
    c2_ref[1:n, :] = e_ref[1:n, :] + e_ref[0:n - 1, :]
    c4_ref[3:n, :] = c2_ref[3:n, :] + c2_ref[1:n - 2, :]
    c8_ref[7:n, :] = c4_ref[7:n, :] + c4_ref[3:n - 4, :]
    c16 = c8_ref[16:n, :] + c8_ref[8:n - 8, :]
    lane = lax.broadcasted_iota(jnp.int32, (t, POOL_WIDTH), 1)
    win = jnp.where(lane < 64, c2_ref[16:n, :],
                    jnp.where(lane < 128, c4_ref[16:n, :], jnp.where(lane < 192, c8_ref[16:n, :], c16)))
    wlen = jnp.where(lane < 64, 2, jnp.where(lane < 128, 4, jnp.where(lane < 192, 8, 16)))
    if full_count:
        cnt = wlen
    else:
        pos = i * t + lax.broadcasted_iota(jnp.int32, (t, POOL_WIDTH), 0)
        cnt = jnp.minimum(pos + 1, wlen)
    d = win / cnt.astype(F32) - p
    o_ref[...] = (_dotb(d, w_ref[...]) * sc_ref[...]).astype(BF16)


def _pool(pa, nseq, w, full_count):
    n = pa.shape[0]
    s = n // nseq
    t = _tile(s, 512)
    nt = s // t
    buf = pltpu.VMEM((t + 16, POOL_WIDTH), F32)
    return pl.pallas_call(
        functools.partial(_pool_kernel, t=t, full_count=full_count),
        grid=(nseq, nt),
        in_specs=[pl.BlockSpec((t, POOL_WIDTH), lambda b, i: (b * nt + i, 0)),
                  _full((POOL_WIDTH, POOL_WIDTH)), _full((1, POOL_WIDTH))],
        out_specs=pl.BlockSpec((t, POOL_WIDTH), lambda b, i: (b * nt + i, 0)),
        out_shape=jax.ShapeDtypeStruct((n, POOL_WIDTH), BF16),
        scratch_shapes=[buf, buf, buf, buf],
        compiler_params=_cparams(("arbitrary", "arbitrary")),
        name="pool",
    )(pa, w["pool_w"], w["pool_scale"])


def _rwkv_pre_kernel(*refs, t, has_vres):
    if has_vres:
        (pb_ref, mu_ref, wl_ref, w0_ref, a0_ref, kkp_ref, ka_ref, bd_ref, vf_ref, v0_ref, v1_ref, v2_ref,
         r_ref, lw_ref, k_ref, v_ref, kk_ref, a_ref, g_ref, e_ref) = refs
    else:
        (pb_ref, mu_ref, wl_ref, w0_ref, a0_ref, kkp_ref, ka_ref, bd_ref,
         r_ref, lw_ref, k_ref, v_ref, kk_ref, a_ref, g_ref, e_ref) = refs
    i = pl.program_id(1)

    @pl.when(i == 0)
    def _():
        e_ref[0:8, :] = jnp.zeros((8, RWKV_COLS), F32)

    @pl.when(i > 0)
    def _():
        e_ref[0:8, :] = e_ref[t:t + 8, :]

    pb = pb_ref[...]
    e_ref[8:t + 8, :] = pb
    prev = e_ref[7:t + 7, :]
    xs = pb + (prev - pb) * mu_ref[...]
    r = xs[:, 0:256]
    k = xs[:, 256:512]
    v = xs[:, 512:768]
    lo = xs[:, 768:832]
    lane = lax.broadcasted_iota(jnp.int32, lo.shape, 1)
    act = jnp.where(lane < 16, jnp.tanh(lo), jnp.where(lane < 32, lo, _sigmoid(lo)))
    lora = _dot3(act, wl_ref[...])
    z = -(w0_ref[...] + lora[:, 0:256])
    softplus = jnp.maximum(z, 0.0) + jnp.log(1.0 + jnp.exp(-jnp.abs(z)))
    lw_ref[...] = -jnp.exp(-softplus - 0.5)
    if has_vres:
        vf = vf_ref[...]
        mix = _dot3(_dot3(v, v1_ref[...]), v2_ref[...])
        v = v + (vf - v) * _sigmoid(v0_ref[...] + mix)
    a = _sigmoid(a0_ref[...] + lora[:, 256:512])
    kk = k * kkp_ref[...]
    ss = _dot3(kk * kk, bd_ref[...])
    kk = kk / jnp.maximum(jnp.sqrt(ss), 1e-12)
    r_ref[...] = r
    k_ref[...] = k * (1.0 + (a - 1.0) * ka_ref[...])
    v_ref[...] = v
    kk_ref[...] = kk
    a_ref[...] = a
    g_ref[...] = lora[:, 512:768]


def _rwkv_pre(pb, nseq, w, vfirst):
    n = pb.shape[0]
    s = n // nseq
    t = _tile(s, 256)
    nt = s // t
    has_vres = vfirst is not None
    row = lambda c: pl.BlockSpec((t, c), lambda b, i: (b * nt + i, 0))
    vec = _full((1, RWKV_WIDTH))
    in_specs = [row(RWKV_COLS), _full((1, RWKV_COLS)), _full((64, 768)), vec, vec, vec, vec,
                _full((RWKV_WIDTH, RWKV_WIDTH))]
    args = [pb, w["mu"], w["lora"], w["w0"], w["a0"], w["kkp"], w["ka"], w["bd"]]
    if has_vres:
        in_specs += [row(RWKV_WIDTH), vec, _full((RWKV_WIDTH, LANE)), _full((LANE, RWKV_WIDTH))]
        args += [vfirst, w["v0"], w["v1"], w["v2"]]
    return pl.pallas_call(
        functools.partial(_rwkv_pre_kernel, t=t, has_vres=has_vres),
        grid=(nseq, nt),
        in_specs=in_specs,
        out_specs=[row(RWKV_WIDTH)] * 7,
        out_shape=[jax.ShapeDtypeStruct((n, RWKV_WIDTH), F32)] * 7,
        scratch_shapes=[pltpu.VMEM((t + 8, RWKV_COLS), F32)],
        compiler_params=_cparams(("arbitrary", "arbitrary")),
        name="rwkv_pre",
    )(*args)


def _scan_chunk(r, lw, k, v, kk, a, st, c):
    hc = RWKV_HEADS * c
    lane = lax.broadcasted_iota(jnp.int32, (c, RWKV_WIDTH), 1)
    ri = lax.broadcasted_iota(jnp.int32, (c, c), 0)
    ci = lax.broadcasted_iota(jnp.int32, (c, c), 1)
    tril = jnp.where(ri >= ci, 1.0, 0.0).astype(F32)
    cum = jnp.dot(tril, lw, preferred_element_type=F32, precision=lax.Precision.HIGHEST)
    tot = cum[c - 1:c, :]
    e_neg = jnp.exp(-cum)
    at = -kk * jnp.exp(cum - lw)
    bt = kk * a * e_neg
    kt = k * e_neg
    rt = r * jnp.exp(cum)
    e_rem = jnp.exp(tot - cum)
    bh = kk * a * e_rem
    kh = k * e_rem

    def bd(x):
        return jnp.concatenate([jnp.where((lane >= h * RWKV_HEAD) & (lane < (h + 1) * RWKV_HEAD), x, 0.0)
                                for h in range(RWKV_HEADS)], axis=0)

    lst = jnp.concatenate([bd(at), bd(rt)], axis=0)
    rst = jnp.concatenate([bd(bt), bd(kt)], axis=0)
    g = _dot3(lst, rst, NT)
    gr = lax.broadcasted_iota(jnp.int32, (hc, hc), 0) % c
    gc = lax.broadcasted_iota(jnp.int32, (hc, hc), 1) % c
    strict = gc < gr
    incl = gc <= gr
    n_ab = jnp.where(strict, g[0:hc, 0:hc], 0.0)
    a_ak = jnp.where(strict, g[0:hc, hc:2 * hc], 0.0)
    a_rb = jnp.where(incl, g[hc:2 * hc, 0:hc], 0.0)
    a_rk = jnp.where(incl, g[hc:2 * hc, hc:2 * hc], 0.0)
    eye = (lax.broadcasted_iota(jnp.int32, (hc, hc), 0) == lax.broadcasted_iota(jnp.int32, (hc, hc), 1))
    inv = jnp.where(eye, 1.0, 0.0) + n_ab
    npow = n_ab
    for it in range(int(math.log2(c)) - 1):
        dot = _dot3 if it < 1 else _dotb
        npow = dot(npow, npow)
        inv = inv + dot(inv, npow)
    w12 = _dot3(lst, st)
    vr = jnp.concatenate([v[:, h * RWKV_HEAD:(h + 1) * RWKV_HEAD] for h in range(RWKV_HEADS)], axis=0)
    u = _dot3(inv, w12[0:hc] + _dot3(a_ak, vr))
    yr = w12[hc:2 * hc] + _dot3(a_rb, u) + _dot3(a_rk, vr)
    y = jnp.concatenate([yr[h * c:(h + 1) * c, :] for h in range(RWKV_HEADS)], axis=1)
    e256r = lax.broadcasted_iota(jnp.int32, (RWKV_WIDTH, RWKV_WIDTH), 0)
    e256c = lax.broadcasted_iota(jnp.int32, (RWKV_WIDTH, RWKV_WIDTH), 1)
    dec = jnp.where(e256r == e256c, jnp.broadcast_to(jnp.exp(tot), (RWKV_WIDTH, RWKV_WIDTH)), 0.0)
    lhs = jnp.concatenate([dec, bd(bh), bd(kh)], axis=0)
    rhs = jnp.concatenate([st, u, vr], axis=0)
    return y, _dot3(lhs, rhs, TN)


def _scan_kernel(r_ref, lw_ref, k_ref, v_ref, kk_ref, a_ref, s0_ref, y_ref, sf_ref, st_ref, *, c, nch, nb):
    j = pl.program_id(1)

    @pl.when(j == 0)
    def _():
        st_ref[...] = s0_ref[...]

    def body(ch, carry):
        sl = pl.ds(pl.multiple_of(ch * c, c), c)
        for i in range(nb):
            y, st = _scan_chunk(r_ref[i, sl, :], lw_ref[i, sl, :], k_ref[i, sl, :], v_ref[i, sl, :],
                                kk_ref[i, sl, :], a_ref[i, sl, :], st_ref[i], c)
            y_ref[i, sl, :] = y
            st_ref[i] = st
        return carry

    lax.fori_loop(0, nch, body, 0)

    @pl.when(j == pl.num_programs(1) - 1)
    def _():
        sf_ref[...] = st_ref[...]


def _scan(r, lw, k, v, kk, a, s0, c):
    n = r.shape[0]
    nseq = s0.shape[0]
    s = n // nseq
    nb = 2 if s > c else math.gcd(nseq, 8)
    nch = max(1, min(8, s // c))
    while (s // c) % nch:
        nch -= 1
    t = c * nch
    nt = s // t
    row = pl.BlockSpec((nb, t, RWKV_WIDTH), lambda b, i: (b, i, 0))
    st_spec = pl.BlockSpec((nb, RWKV_WIDTH, RWKV_HEAD), lambda b, i: (b, 0, 0))
    seqs = [x.reshape(nseq, s, RWKV_WIDTH) for x in (r, lw, k, v, kk, a)]
    y, sfin = pl.pallas_call(
        functools.partial(_scan_kernel, c=c, nch=nch, nb=nb),
        grid=(nseq // nb, nt),
        in_specs=[row] * 6 + [st_spec],
        out_specs=[row, st_spec],
        out_shape=[jax.ShapeDtypeStruct((nseq, s, RWKV_WIDTH), F32),
                   jax.ShapeDtypeStruct((nseq, RWKV_WIDTH, RWKV_HEAD), F32)],
        scratch_shapes=[pltpu.VMEM((nb, RWKV_WIDTH, RWKV_HEAD), F32)],
        compiler_params=_cparams(("arbitrary", "arbitrary")),
        name="wkv_scan",
    )(*seqs, s0)
    return y.reshape(n, RWKV_WIDTH), sfin


def _latent_to_heads(o, wuv_ref, tq):
    out = None
    for h in range(MLA_HEADS):
        part = _dotb(o[h * tq:(h + 1) * tq, :], wuv_ref[h])
        out = part if out is None else out + part
    return out


CHAIN_ROWS = 1024


def _attn_prompt_kernel(qi_ref, ki_ref, q_ref, k_ref, wuv_ref, o_ref, m_ref, l_ref, acc_ref, *, tq, tk):
    step = pl.program_id(1)
    qi = qi_ref[step]
    ki = ki_ref[step]
    rows = MLA_HEADS * tq

    @pl.when(ki == 0)
    def _():
        m_ref[...] = jnp.full((rows, 1), NEG, F32)
        l_ref[...] = jnp.zeros((rows, LANE), F32)
        acc_ref[...] = jnp.zeros((rows, MLA_KV_RANK), F32)

    last = (qi * tq + tq - 1) // tk
    cheads = max(1, CHAIN_ROWS // tq)
    crows = cheads * tq

    def update(masked):
        k = k_ref[...]
        kv = k[:, :MLA_KV_RANK]
        if masked:
            rel = (lax.broadcasted_iota(jnp.int32, (crows, tk), 1)
                   - lax.broadcasted_iota(jnp.int32, (crows, tk), 0) % tq)
            visible = rel <= qi * tq - ki * tk
        for c in range(MLA_HEADS // cheads):
            r0 = c * crows
            q = q_ref[c * cheads:(c + 1) * cheads].reshape(crows, KCAT)
            s = lax.dot_general(q, k, NT, preferred_element_type=F32)
            if masked:
                s = jnp.where(visible, s, NEG)
            m_old = m_ref[r0:r0 + crows, :]
            m_new = jnp.maximum(m_old, jnp.max(s, axis=-1, keepdims=True))
            alpha = jnp.exp2(m_old - m_new)
            p = jnp.exp2(s - m_new)
            psum = p[:, 0:LANE]
            for t in range(1, tk // LANE):
                psum = psum + p[:, t * LANE:(t + 1) * LANE]
            l_ref[r0:r0 + crows, :] = alpha * l_ref[r0:r0 + crows, :] + psum
            acc_ref[r0:r0 + crows, :] = alpha * acc_ref[r0:r0 + crows, :] + jnp.dot(
                p.astype(BF16), kv, preferred_element_type=F32)
            m_ref[r0:r0 + crows, :] = m_new

    @pl.when(ki < last)
    def _():
        update(False)

    @pl.when(ki == last)
    def _():
        update(True)
        o = acc_ref[...] / jnp.sum(l_ref[...], axis=-1, keepdims=True)
        o_ref[...] = _latent_to_heads(o, wuv_ref, tq).astype(BF16)


def _attn_prompt(qcat, kcat, wuv, nseq):
    n = kcat.shape[0]
    s = n // nseq
    tq = _tile(s, 512)
    tk = _tile(s, 1024)
    nq, nk = s // tq, s // tk
    assert tk % LANE == 0 and tk % tq == 0
    pairs = [(qi, ki) for qi in range(nq) for ki in range((qi * tq + tq - 1) // tk + 1)]
    qi_tab = jnp.asarray(np.array([p[0] for p in pairs], np.int32))
    ki_tab = jnp.asarray(np.array([p[1] for p in pairs], np.int32))
    rows = MLA_HEADS * tq
    grid_spec = pltpu.PrefetchScalarGridSpec(
        num_scalar_prefetch=2,
        grid=(nseq, len(pairs)),
        in_specs=[pl.BlockSpec((MLA_HEADS, tq, KCAT), lambda b, p, qt, kt: (0, b * nq + qt[p], 0)),
                  pl.BlockSpec((tk, KCAT), lambda b, p, qt, kt: (b * nk + kt[p], 0)),
                  pl.BlockSpec((MLA_HEADS, MLA_KV_RANK, MLA_HEADS * MLA_V_DIM), lambda b, p, qt, kt: (0, 0, 0))],
        out_specs=pl.BlockSpec((tq, MLA_HEADS * MLA_V_DIM), lambda b, p, qt, kt: (b * nq + qt[p], 0)),
        scratch_shapes=[pltpu.VMEM((rows, 1), F32), pltpu.VMEM((rows, LANE), F32),
                        pltpu.VMEM((rows, MLA_KV_RANK), F32)],
    )
    return pl.pallas_call(
        functools.partial(_attn_prompt_kernel, tq=tq, tk=tk),
        grid_spec=grid_spec,
        out_shape=jax.ShapeDtypeStruct((n, MLA_HEADS * MLA_V_DIM), BF16),
        compiler_params=_cparams(("parallel", "arbitrary")),
        name="attn_prompt",
    )(qi_tab, ki_tab, qcat, kcat, wuv)


def _attn_sample_kernel(pt_ref, q_ref, kn_ref, wuv_ref, ckv_hbm, krt_hbm, o_ref, cbuf, rbuf, sems,
                        *, layer, ppc, nchunk, sd):
    b = pl.program_id(0)
    nb = pl.num_programs(0)
    rows = MLA_HEADS * sd

    def copies(bb, chunk, slot):
        out = []
        for i in range(ppc):
            page = pt_ref[bb, chunk * ppc + i]
            out.append(pltpu.make_async_copy(ckv_hbm.at[layer, page], cbuf.at[slot, i], sems.at[0, slot]))
            out.append(pltpu.make_async_copy(krt_hbm.at[layer, page],
                                             rbuf.at[slot, :, pl.ds(i * PAGE_SIZE, PAGE_SIZE)], sems.at[1, slot]))
        return out

    @pl.when(b == 0)
    def _():
        for cp in copies(0, 0, 0):
            cp.start()

    q = q_ref[...].reshape(rows, KCAT)
    ql = q[:, :MLA_KV_RANK]
    qr = q[:, MLA_KV_RANK:MLA_KV_RANK + MLA_ROPE]

    def accumulate(state, s, values):
        m, l, acc = state
        m_new = jnp.maximum(m, jnp.max(s, axis=-1, keepdims=True))
        alpha = jnp.exp2(m - m_new)
        p = jnp.exp2(s - m_new)
        l = alpha * l + jnp.sum(p, axis=-1, keepdims=True)
        acc = alpha * acc + jnp.dot(p.astype(BF16), values, preferred_element_type=F32)
        return m_new, l, acc

    state = (jnp.full((rows, 1), NEG, F32), jnp.zeros((rows, 1), F32), jnp.zeros((rows, MLA_KV_RANK), F32))
    for c in range(nchunk):
        slot = c % 2
        if c + 1 < nchunk:
            for cp in copies(b, c + 1, 1 - slot):
                cp.start()
        else:
            @pl.when(b + 1 < nb)
            def _():
                for cp in copies(b + 1, 0, 1 - slot):
                    cp.start()
        for cp in copies(b, c, slot):
            cp.wait()
        kc = cbuf[slot].reshape(ppc * PAGE_SIZE, MLA_KV_RANK).astype(BF16)
        krt = rbuf[slot].astype(BF16)
        s = (lax.dot_general(ql, kc, NT, preferred_element_type=F32)
             + jnp.dot(qr, krt, preferred_element_type=F32))
        state = accumulate(state, s, kc)
    kn = kn_ref[...]
    s = lax.dot_general(q, kn, NT, preferred_element_type=F32)
    qpos = lax.broadcasted_iota(jnp.int32, (rows, sd), 0) % sd
    kpos = lax.broadcasted_iota(jnp.int32, (rows, sd), 1)
    s = jnp.where(kpos <= qpos, s, NEG)
    _, l, acc = accumulate(state, s, kn[:, :MLA_KV_RANK])
    o_ref[...] = _latent_to_heads(acc / l, wuv_ref, sd).astype(BF16)


def _attn_sample(qcat, kcat, wuv, cache_ckv, cache_krope_t, page_table, layer):
    nb, n_pages = page_table.shape
    n = kcat.shape[0]
    sd = n // nb
    ppc = max(1, min(16, n_pages // 2))
    nchunk = n_pages // ppc
    assert n_pages % ppc == 0 and nchunk % 2 == 0 and sd % 8 == 0 and sd & (sd - 1) == 0
    grid_spec = pltpu.PrefetchScalarGridSpec(
        num_scalar_prefetch=1,
        grid=(nb,),
        in_specs=[pl.BlockSpec((MLA_HEADS, sd, KCAT), lambda b, pt: (0, b, 0)),
                  pl.BlockSpec((sd, KCAT), lambda b, pt: (b, 0)),
                  pl.BlockSpec((MLA_HEADS, MLA_KV_RANK, MLA_HEADS * MLA_V_DIM), lambda b, pt: (0, 0, 0)),
                  pl.BlockSpec(memory_space=pl.ANY), pl.BlockSpec(memory_space=pl.ANY)],
        out_specs=pl.BlockSpec((sd, MLA_HEADS * MLA_V_DIM), lambda b, pt: (b, 0)),
        scratch_shapes=[pltpu.VMEM((2, ppc, PAGE_SIZE, MLA_KV_RANK), F32),
                        pltpu.VMEM((2, MLA_ROPE, ppc * PAGE_SIZE), F32),
                        pltpu.SemaphoreType.DMA((2, 2))],
    )
    return pl.pallas_call(
        functools.partial(_attn_sample_kernel, layer=layer, ppc=ppc, nchunk=nchunk, sd=sd),
        grid_spec=grid_spec,
        out_shape=jax.ShapeDtypeStruct((n, MLA_HEADS * MLA_V_DIM), BF16),
        compiler_params=_cparams(("arbitrary",)),
        name="attn_sample",
    )(page_table, qcat, kcat, wuv, cache_ckv, cache_krope_t)


def _post_kernel(y_ref, r_ref, k_ref, v_ref, g_ref, ya_ref, yc_ref, x_ref, rk_ref, lg_ref, lb_ref, bd_ref,
                 wo_ref, g1_ref, b1_ref, o_ref):
    y = y_ref[...]
    bd = bd_ref[...]
    inv = 1.0 / RWKV_HEAD
    mu = _dot3(y, bd) * inv
    yc = y - mu
    var = _dot3(yc * yc, bd) * inv
    yn = yc * lax.rsqrt(var + RWKV_GN_EPS) * lg_ref[...] + lb_ref[...]
    bonus = _dot3(r_ref[...] * k_ref[...] * rk_ref[...], bd) * v_ref[...]
    yb = (yn + bonus) * g_ref[...]
    h = (jnp.dot(ya_ref[...], wo_ref[0:256, :], preferred_element_type=F32)
         + _dotb(yb, wo_ref[256:512, :])
         + jnp.dot(yc_ref[...], wo_ref[512:1024, :], preferred_element_type=F32))
    o_ref[...] = _layer_norm(DEEPNORM_ALPHA * x_ref[...] + h, g1_ref[...], b1_ref[...])


def _post(y, r, k, v, g, ya, yc, x, w):
    n = x.shape[0]
    t = _tile(n, 256)
    row = lambda c: pl.BlockSpec((t, c), lambda i: (i, 0))
    vec = _full((1, RWKV_WIDTH))
    vecd = _full((1, D_MODEL))
    return pl.pallas_call(
        _post_kernel,
        grid=(n // t,),
        in_specs=[row(256)] * 6 + [row(512), row(D_MODEL), vec, vec, vec, _full((256, 256)),
                                   _full((D_MODEL, D_MODEL)), vecd, vecd],
        out_specs=row(D_MODEL),
        out_shape=jax.ShapeDtypeStruct((n, D_MODEL), F32),
        compiler_params=_cparams(("parallel",)),
        name="post",
    )(y, r, k, v, g, ya, yc, x, w["rk"], w["lnx_g"], w["lnx_b"], w["bd"], w["wo"], w["ln1_g"], w["ln1_b"])


def _ffn_kernel(x_ref, wg_ref, wu_ref, wd_ref, g2_ref, b2_ref, o_ref, acc_ref):
    f = pl.program_id(1)

    @pl.when(f == 0)
    def _():
        acc_ref[...] = jnp.zeros(acc_ref.shape, F32)

    xb = x_ref[...].astype(BF16)
    g = jnp.dot(xb, wg_ref[...], preferred_element_type=F32)
    u = jnp.dot(xb, wu_ref[...], preferred_element_type=F32)
    hmid = g * _sigmoid(g) * u
    acc_ref[...] += jnp.dot(hmid.astype(BF16), wd_ref[...], preferred_element_type=F32)

    @pl.when(f == pl.num_programs(1) - 1)
    def _():
        o_ref[...] = _layer_norm(DEEPNORM_ALPHA * x_ref[...] + acc_ref[...], g2_ref[...], b2_ref[...])


def _ffn(x, w):
    n = x.shape[0]
    dff = w["ffn_g"].shape[1]
    tm = _tile(n, 1024)
    tf = 256
    vecd = _full((1, D_MODEL))
    return pl.pallas_call(
        _ffn_kernel,
        grid=(n // tm, dff // tf),
        in_specs=[pl.BlockSpec((tm, D_MODEL), lambda i, f: (i, 0)),
                  pl.BlockSpec((D_MODEL, tf), lambda i, f: (0, f)),
                  pl.BlockSpec((D_MODEL, tf), lambda i, f: (0, f)),
                  pl.BlockSpec((tf, D_MODEL), lambda i, f: (f, 0)), vecd, vecd],
        out_specs=pl.BlockSpec((tm, D_MODEL), lambda i, f: (i, 0)),
        out_shape=jax.ShapeDtypeStruct((n, D_MODEL), F32),
        scratch_shapes=[pltpu.VMEM((tm, D_MODEL), F32)],
        compiler_params=_cparams(("parallel", "arbitrary")),
        name="ffn",
    )(x, w["ffn_g"], w["ffn_u"], w["ffn_d"], w["ln2_g"], w["ln2_b"])


MOE_TILE = 512
IDX_BLOCK = 1024
TOP_K = 2


def _router_kernel(x_ref, rt_ref, tri_ref, meta_ref, cnt_ref, run_ref):
    i = pl.program_id(0)

    @pl.when(i == 0)
    def _():
        run_ref[...] = jnp.zeros(run_ref.shape, F32)

    logits = jnp.dot(x_ref[...], rt_ref[...], preferred_element_type=F32, precision=lax.Precision.HIGHEST)
    lane = lax.broadcasted_iota(jnp.int32, logits.shape, 1).astype(F32)
    logits = jnp.where(lane < N_EXPERTS, logits, NEG)
    m1 = jnp.max(logits, axis=-1, keepdims=True)
    i1 = jnp.min(jnp.where(logits == m1, lane, float(LANE)), axis=-1, keepdims=True)
    rest = jnp.where(lane == i1, NEG, logits)
    m2 = jnp.max(rest, axis=-1, keepdims=True)
    i2 = jnp.min(jnp.where(rest == m2, lane, float(LANE)), axis=-1, keepdims=True)
    e2 = jnp.exp(m2 - m1)
    g1 = 1.0 / (1.0 + e2)
    g2 = e2 * g1
    onehot = jnp.where(lane == i1, 1.0, jnp.where(lane == i2, 1.0, 0.0))
    before = jnp.dot(tri_ref[...], onehot.astype(BF16), preferred_element_type=F32) + run_ref[...]
    r1 = jnp.sum(jnp.where(lane == i1, before, 0.0), axis=-1, keepdims=True)
    r2 = jnp.sum(jnp.where(lane == i2, before, 0.0), axis=-1, keepdims=True)
    run_ref[...] += jnp.sum(onehot, axis=0, keepdims=True)
    meta = jnp.zeros(logits.shape, F32)
    for col, val in enumerate((i1, i2, g1, g2, r1, r2)):
        meta = jnp.where(lane == col, val, meta)
    meta_ref[...] = meta
    cnt_ref[...] = run_ref[...]


def _gather_rows_kernel(src_ref, x_hbm, o_ref, buf, sem):
    def copy(r):
        tok = src_ref[r // LANE, r % LANE]
        return pltpu.make_async_copy(x_hbm.at[pl.ds(tok, 1)], buf.at[pl.ds(r, 1)], sem)

    def start(r, c):
        copy(r).start()
        return c

    def wait(r, c):
        copy(r).wait()
        return c

    lax.fori_loop(0, IDX_BLOCK, start, 0, unroll=8)
    lax.fori_loop(0, IDX_BLOCK, wait, 0, unroll=8)
    o_ref[...] = buf[...].astype(BF16)


def _experts_kernel(te_ref, nv_ref, x_ref, wg_ref, wu_ref, wd_ref, o_ref, acc_ref):
    i = pl.program_id(0)
    f = pl.program_id(1)
    last = pl.num_programs(1) - 1

    @pl.when(i < nv_ref[0])
    def _():
        @pl.when(f == 0)
        def _():
            acc_ref[...] = jnp.zeros(acc_ref.shape, F32)

        xb = x_ref[...]
        g = jnp.dot(xb, wg_ref[0], preferred_element_type=F32)
        u = jnp.dot(xb, wu_ref[0], preferred_element_type=F32)
        hmid = g * _sigmoid(g) * u
        acc_ref[...] += jnp.dot(hmid.astype(BF16), wd_ref[0], preferred_element_type=F32)

        @pl.when(f == last)
        def _():
            o_ref[...] = acc_ref[...]

    @pl.when((i >= nv_ref[0]) & (f == last))
    def _():
        o_ref[...] = jnp.zeros(o_ref.shape, F32)


def _combine_kernel(d1_ref, d2_ref, meta_ref, x_ref, g2_ref, b2_ref, ys_hbm, o_ref, buf1, buf2, sems):
    def copies(r):
        a = d1_ref[r // LANE, r % LANE]
        b = d2_ref[r // LANE, r % LANE]
        return (pltpu.make_async_copy(ys_hbm.at[pl.ds(a, 1)], buf1.at[pl.ds(r, 1)], sems.at[0]),
                pltpu.make_async_copy(ys_hbm.at[pl.ds(b, 1)], buf2.at[pl.ds(r, 1)], sems.at[1]))

    def start(r, c):
        for cp in copies(r):
            cp.start()
        return c

    def wait(r, c):
        for cp in copies(r):
            cp.wait()
        return c

    lax.fori_loop(0, IDX_BLOCK, start, 0, unroll=8)
    lax.fori_loop(0, IDX_BLOCK, wait, 0, unroll=8)
    meta = meta_ref[...]
    y = meta[:, 2:3] * buf1[...] + meta[:, 3:4] * buf2[...]
    o_ref[...] = _layer_norm(DEEPNORM_ALPHA * x_ref[...] + y, g2_ref[...], b2_ref[...])


def _moe(x, w):
    n = x.shape[0]
    ne, _, dff = w["moe_g"].shape
    assert n % IDX_BLOCK == 0 and ne == N_EXPERTS
    tr = IDX_BLOCK
    meta, cnt = pl.pallas_call(
        _router_kernel,
        grid=(n // tr,),
        in_specs=[pl.BlockSpec((tr, D_MODEL), lambda i: (i, 0)), _full((D_MODEL, LANE)), _full((tr, tr))],
        out_specs=[pl.BlockSpec((tr, LANE), lambda i: (i, 0)), _full((1, LANE))],
        out_shape=[jax.ShapeDtypeStruct((n, LANE), F32), jax.ShapeDtypeStruct((1, LANE), F32)],
        scratch_shapes=[pltpu.VMEM((1, LANE), F32)],
        compiler_params=_cparams(("arbitrary",)),
        name="moe_router",
    )(x, w["router"], w["tri"])

    i1 = meta[:, 0].astype(jnp.int32)
    i2 = meta[:, 1].astype(jnp.int32)
    counts = cnt[0, :ne].astype(jnp.int32)
    padded = (counts + MOE_TILE - 1) // MOE_TILE * MOE_TILE
    ends = jnp.cumsum(padded)
    starts = ends - padded
    d1 = starts[i1] + meta[:, 4].astype(jnp.int32)
    d2 = starts[i2] + meta[:, 5].astype(jnp.int32)
    a_pad = TOP_K * n + ne * MOE_TILE
    tok = jnp.arange(n, dtype=jnp.int32)
    src = jnp.zeros((a_pad,), jnp.int32).at[d1].set(tok).at[d2].set(tok)
    n_tiles = a_pad // MOE_TILE
    tile_expert = jnp.minimum(jnp.searchsorted(ends, jnp.arange(n_tiles, dtype=jnp.int32) * MOE_TILE,
                                               side="right"), ne - 1).astype(jnp.int32)
    n_valid = (ends[-1] // MOE_TILE).astype(jnp.int32).reshape(1)

    idx_spec = pl.BlockSpec((IDX_BLOCK // LANE, LANE), lambda i: (i, 0), memory_space=pltpu.SMEM)
    xs = pl.pallas_call(
        _gather_rows_kernel,
        grid=(a_pad // IDX_BLOCK,),
        in_specs=[idx_spec, pl.BlockSpec(memory_space=pl.ANY)],
        out_specs=pl.BlockSpec((IDX_BLOCK, D_MODEL), lambda i: (i, 0)),
        out_shape=jax.ShapeDtypeStruct((a_pad, D_MODEL), BF16),
        scratch_shapes=[pltpu.VMEM((IDX_BLOCK, D_MODEL), F32), pltpu.SemaphoreType.DMA(())],
        compiler_params=_cparams(("arbitrary",)),
        name="moe_gather",
    )(src.reshape(a_pad // LANE, LANE), x)

    tf = 896 if dff % 896 == 0 else 512
    ys = pl.pallas_call(
        _experts_kernel,
        grid_spec=pltpu.PrefetchScalarGridSpec(
            num_scalar_prefetch=2,
            grid=(n_tiles, dff // tf),
            in_specs=[pl.BlockSpec((MOE_TILE, D_MODEL), lambda i, f, te, nv: (i, 0)),
                      pl.BlockSpec((1, D_MODEL, tf), lambda i, f, te, nv: (te[i], 0, f)),
                      pl.BlockSpec((1, D_MODEL, tf), lambda i, f, te, nv: (te[i], 0, f)),
                      pl.BlockSpec((1, tf, D_MODEL), lambda i, f, te, nv: (te[i], f, 0))],
            out_specs=pl.BlockSpec((MOE_TILE, D_MODEL), lambda i, f, te, nv: (i, 0)),
            scratch_shapes=[pltpu.VMEM((MOE_TILE, D_MODEL), F32)],
        ),
        out_shape=jax.ShapeDtypeStruct((a_pad, D_MODEL), F32),
        compiler_params=_cparams(("arbitrary", "arbitrary")),
        name="moe_experts",
    )(tile_expert, n_valid, xs, w["moe_g"], w["moe_u"], w["moe_d"])

    vecd = _full((1, D_MODEL))
    row = lambda c: pl.BlockSpec((IDX_BLOCK, c), lambda i: (i, 0))
    return pl.pallas_call(
        _combine_kernel,
        grid=(n // IDX_BLOCK,),
        in_specs=[idx_spec, idx_spec, row(LANE), row(D_MODEL), vecd, vecd, pl.BlockSpec(memory_space=pl.ANY)],
        out_specs=row(D_MODEL),
        out_shape=jax.ShapeDtypeStruct((n, D_MODEL), F32),
        scratch_shapes=[pltpu.VMEM((IDX_BLOCK, D_MODEL), F32), pltpu.VMEM((IDX_BLOCK, D_MODEL), F32),
                        pltpu.SemaphoreType.DMA((2,))],
        compiler_params=_cparams(("arbitrary",)),
        name="moe_combine",
    )(d1.reshape(n // LANE, LANE), d2.reshape(n // LANE, LANE), meta, x, w["ln2_g"], w["ln2_b"], ys)


_RWKV_PERM = np.concatenate([np.arange(0, 256), np.arange(272, 528), np.arange(528, 784),
                             np.arange(256, 272), np.arange(784, 800), np.arange(800, 832)])
_RWKV_INV = np.argsort(_RWKV_PERM)


def _rot_cols(wr):
    half = MLA_ROPE // 2
    return jnp.concatenate([-wr[..., half:], wr[..., :half]], axis=-1)


def _pad_last(x, width):
    return jnp.pad(x, [(0, 0)] * (x.ndim - 1) + [(0, width - x.shape[-1])])


def _prep_layer(l, p):
    w = {}
    w_in = p["w_in"][l]
    rw = w_in[:, 256:256 + RWKV_COLS][:, _RWKV_PERM]
    mla = w_in[:, 256 + RWKV_COLS:]
    wkr = mla[:, MLA_Q_RANK + MLA_KV_RANK:]
    win = jnp.concatenate([w_in[:, :256], mla[:, :MLA_Q_RANK + MLA_KV_RANK], _pad_last(wkr, LANE),
                           _pad_last(_rot_cols(wkr), LANE), rw], axis=1)
    w["win"] = _pad_last(win, PIN_PAD).astype(BF16)
    wuq = p["mla_wuq"][l]
    wq_rope = wuq[:, :, MLA_NOPE:]
    w["wq"] = jnp.concatenate([_pad_last(wuq[:, :, :MLA_NOPE], LANE).reshape(MLA_Q_RANK, -1),
                               _pad_last(wq_rope, LANE).reshape(MLA_Q_RANK, -1),
                               _pad_last(_rot_cols(wq_rope), LANE).reshape(MLA_Q_RANK, -1)], axis=1).astype(BF16)
    wuk = jnp.transpose(p["mla_wuk"][l], (1, 2, 0))
    w["wuk"] = jnp.pad(wuk, ((0, 0), (0, LANE - MLA_NOPE), (0, 0))).astype(BF16)
    wuv = jnp.transpose(p["mla_wuv"][l], (1, 0, 2))
    eye = jnp.eye(MLA_HEADS, dtype=F32)
    w["wuv"] = (wuv[:, :, None, :] * eye[:, None, :, None]).reshape(MLA_HEADS, MLA_KV_RANK, -1).astype(BF16)
    w["qg"] = p["mla_qnorm_g"][l][None]
    w["kvg"] = p["mla_kvnorm_g"][l][None]
    pw = p["pool_w"][l]
    eye4 = jnp.eye(4, dtype=F32)
    w["pool_w"] = (pw[:, :, None, :] * eye4[:, None, :, None]).reshape(POOL_WIDTH, POOL_WIDTH).astype(BF16)
    w["pool_scale"] = p["pool_scale"][l][None]
    w["mu"] = p["rwkv_mu"][l][_RWKV_PERM][None]
    lora = jnp.zeros((64, 768), F32)
    lora = lora.at[0:16, 0:256].set(p["rwkv_w2"][l]).at[16:32, 256:512].set(p["rwkv_a2"][l])
    w["lora"] = lora.at[32:64, 512:768].set(p["rwkv_g2"][l])
    for name, key in (("w0", "rwkv_w0"), ("a0", "rwkv_a0"), ("kkp", "rwkv_kk"), ("ka", "rwkv_ka"),
                      ("lnx_g", "rwkv_lnx_g"), ("lnx_b", "rwkv_lnx_b")):
        w[name] = p[key][l][None]
    w["rk"] = p["rwkv_rk"][l].reshape(1, RWKV_WIDTH)
    w["bd"] = jnp.kron(jnp.eye(RWKV_HEADS, dtype=F32), jnp.ones((RWKV_HEAD, RWKV_HEAD), F32))
    if l > 0:
        w["v0"] = p["vres_v0"][l - 1][None]
        w["v1"] = _pad_last(p["vres_w1"][l - 1], LANE)
        w["v2"] = jnp.pad(p["vres_w2"][l - 1], ((0, LANE - p["vres_w2"].shape[1]), (0, 0)))
    w["wo"] = p["w_out"][l].astype(BF16)
    for name in ("ln1_g", "ln1_b", "ln2_g", "ln2_b"):
        w[name] = p[name][l][None]
    if l % 2 == 0:
        w["ffn_g"] = p["ffn_w_gate"][l // 2].astype(BF16)
        w["ffn_u"] = p["ffn_w_up"][l // 2].astype(BF16)
        w["ffn_d"] = p["ffn_w_down"][l // 2].astype(BF16)
    else:
        w["router"] = _pad_last(p["moe_router"][l // 2], LANE)
        tri = np.tril(np.ones((IDX_BLOCK, IDX_BLOCK), np.float32), -1)
        w["tri"] = jnp.asarray(tri, BF16)
        w["moe_g"] = p["moe_w_gate"][l // 2].astype(BF16)
        w["moe_u"] = p["moe_w_up"][l // 2].astype(BF16)
        w["moe_d"] = p["moe_w_down"][l // 2].astype(BF16)
    return w


def _rope_tables(pos):
    half = MLA_ROPE // 2
    freqs = ROPE_BASE ** (-jnp.arange(half, dtype=F32) / half)
    ang = pos.astype(F32)[:, None] * freqs[None, :]
    cos = jnp.cos(ang)
    sin = jnp.sin(ang)
    return (_pad_last(jnp.concatenate([cos, cos], axis=-1), LANE),
            _pad_last(jnp.concatenate([sin, sin], axis=-1), LANE))


def _state_to_kernel(s):
    return jnp.swapaxes(s, -1, -2).reshape(s.shape[0], RWKV_WIDTH, RWKV_HEAD)


def _state_from_kernel(s):
    return jnp.swapaxes(s.reshape(s.shape[0], RWKV_HEADS, RWKV_HEAD, RWKV_HEAD), -1, -2)


def _channel_mix(x_prompt, x_sample, l, w):
    if l % 2 == 0:
        return _ffn(x_prompt, w), _ffn(x_sample, w)
    n_prompt = x_prompt.shape[0]
    x = _moe(jnp.concatenate([x_prompt, x_sample], axis=0), w)
    return x[:n_prompt], x[n_prompt:]


def _run_prompt(x, layers):
    b, s, _ = x.shape
    xf = x.reshape(b * s, D_MODEL)
    cos, sin = _rope_tables(jnp.tile(jnp.arange(s, dtype=jnp.int32), b))
    vfirst = None
    outs = []
    for l, w in enumerate(layers):
        pa, pb, ckv, krope, kcat, qcat = _proj(xf, cos, sin, w)
        ya = _pool(pa, b, w, full_count=False)
        r, lw, k, v, kk, a, g = _rwkv_pre(pb, b, w, vfirst)
        if vfirst is None:
            vfirst = v
        y, sfin = _scan(r, lw, k, v, kk, a, jnp.zeros((b, RWKV_WIDTH, RWKV_HEAD), F32), c=_tile(s, 64))
        yc = _attn_prompt(qcat, kcat, w["wuv"], b)
        xf = yield _post(y, r, k, v, g, ya, yc, xf, w)
        outs.append((ckv.reshape(b, s, -1), krope.reshape(b, s, -1), pa.reshape(b, s, -1)[:, -POOL_BUF:],
                     pb.reshape(b, s, -1)[:, -1][:, _RWKV_INV], _state_from_kernel(sfin)))
    yield (xf.reshape(b, s, D_MODEL),) + tuple(jnp.stack([o[i] for o in outs]) for i in range(5))


def _run_sample(x, past_len, cache_ckv, cache_krope, state_pool, state_shift, state_wkv, page_table, layers):
    b, s, _ = x.shape
    xf = x.reshape(b * s, D_MODEL)
    cos, sin = _rope_tables(jnp.tile(past_len + jnp.arange(s, dtype=jnp.int32), b))
    pool_lead = -(POOL_BUF + s) % 8
    shift_lead = 7
    cache_krope_t = jnp.swapaxes(cache_krope, 2, 3)
    vfirst = None
    outs = []
    for l, w in enumerate(layers):
        pa, pb, ckv, krope, kcat, qcat = _proj(xf, cos, sin, w)
        ext = jnp.concatenate([state_pool[l], pa.reshape(b, s, -1)], axis=1)
        ext_p = jnp.pad(ext, ((0, 0), (pool_lead, 0), (0, 0)))
        ya = _pool(ext_p.reshape(-1, POOL_WIDTH), 1, w, full_count=True)
        ya = ya.reshape(b, -1, POOL_WIDTH)[:, -s:].reshape(b * s, POOL_WIDTH)
        pbe = jnp.concatenate([state_shift[l][:, None, _RWKV_PERM], pb.reshape(b, s, -1)], axis=1)
        pbe = jnp.pad(pbe, ((0, 0), (shift_lead, 0), (0, 0)))
        pre = _rwkv_pre(pbe.reshape(-1, RWKV_COLS), 1, w, vfirst)
        if vfirst is None:
            vfirst = pre[3]
        r, lw, k, v, kk, a, g = (t.reshape(b, -1, RWKV_WIDTH)[:, -s:].reshape(b * s, RWKV_WIDTH) for t in pre)
        y, sfin = _scan(r, lw, k, v, kk, a, _state_to_kernel(state_wkv[l]), c=s)
        yc = _attn_sample(qcat, kcat, w["wuv"], cache_ckv, cache_krope_t, page_table, l)
        xf = yield _post(y, r, k, v, g, ya, yc, xf, w)
        outs.append((ckv.reshape(b, s, -1), krope.reshape(b, s, -1), ext[:, -POOL_BUF:],
                     pb.reshape(b, s, -1)[:, -1][:, _RWKV_INV], _state_from_kernel(sfin)))
    yield (xf.reshape(b, s, D_MODEL),) + tuple(jnp.stack([o[i] for o in outs]) for i in range(5))


def kernel(x_prompt, x_sample, cache_ckv, cache_krope, state_pool, state_shift, state_wkv, page_table, ln1_g, ln1_b, ln2_g, ln2_b, w_in, pool_w, pool_scale, rwkv_mu, rwkv_w0, rwkv_w2, rwkv_a0, rwkv_a2, rwkv_g2, rwkv_kk, rwkv_ka, rwkv_rk, rwkv_lnx_g, rwkv_lnx_b, vres_v0, vres_w1, vres_w2, mla_qnorm_g, mla_wuq, mla_kvnorm_g, mla_wuk, mla_wuv, w_out, ffn_w_gate, ffn_w_up, ffn_w_down, moe_router, moe_w_gate, moe_w_up, moe_w_down):
    p = dict(ln1_g=ln1_g, ln1_b=ln1_b, ln2_g=ln2_g, ln2_b=ln2_b, w_in=w_in, pool_w=pool_w, pool_scale=pool_scale,
             rwkv_mu=rwkv_mu, rwkv_w0=rwkv_w0, rwkv_w2=rwkv_w2, rwkv_a0=rwkv_a0, rwkv_a2=rwkv_a2, rwkv_g2=rwkv_g2,
             rwkv_kk=rwkv_kk, rwkv_ka=rwkv_ka, rwkv_rk=rwkv_rk, rwkv_lnx_g=rwkv_lnx_g, rwkv_lnx_b=rwkv_lnx_b,
             vres_v0=vres_v0, vres_w1=vres_w1, vres_w2=vres_w2, mla_qnorm_g=mla_qnorm_g, mla_wuq=mla_wuq,
             mla_kvnorm_g=mla_kvnorm_g, mla_wuk=mla_wuk, mla_wuv=mla_wuv, w_out=w_out, ffn_w_gate=ffn_w_gate,
             ffn_w_up=ffn_w_up, ffn_w_down=ffn_w_down, moe_router=moe_router, moe_w_gate=moe_w_gate,
             moe_w_up=moe_w_up, moe_w_down=moe_w_down)
    layers = [_prep_layer(l, p) for l in range(DEPTH)]
    past_len = page_table.shape[1] * PAGE_SIZE
    prompt_trunk = _run_prompt(x_prompt, layers)
    sample_trunk = _run_sample(x_sample, past_len, cache_ckv, cache_krope, state_pool, state_shift, state_wkv,
                               page_table, layers)
    prompt, sample = next(prompt_trunk), next(sample_trunk)
    for l, w in enumerate(layers):
        x_p, x_s = _channel_mix(prompt, sample, l, w)
        prompt, sample = prompt_trunk.send(x_p), sample_trunk.send(x_s)
    return (prompt[0], sample[0]) + prompt[1:] + sample[1:]
```

```python
import functools
import math

import numpy as np
import jax
import jax.numpy as jnp
from jax import lax
from jax.experimental import pallas as pl
from jax.experimental.pallas import tpu as pltpu

F32 = jnp.float32
BF16 = jnp.bfloat16

D_MODEL = 1024
POOL_WIDTH = 256
POOL_WINDOWS = (2, 4, 8, 16)
POOL_GDIM = 64
POOL_BUF = 15
RWKV_HEAD = 64
RWKV_WIDTH = 256
RWKV_HEADS = 4
RWKV_COLS = 832
RWKV_GN_EPS = 64e-5
MLA_HEADS = 8
MLA_V_DIM = 64
MLA_NOPE = 64
MLA_ROPE = 32
MLA_Q_RANK = 384
MLA_KV_RANK = 256
MLA_SCALE = (MLA_NOPE + MLA_ROPE) ** -0.5
QK_SCALE = MLA_SCALE * math.log2(math.e)
ROPE_BASE = 10000.0
PAGE_SIZE = 128
N_EXPERTS = 8
DEPTH = 2
DEEPNORM_ALPHA = (2 * DEPTH) ** 0.25
LN_EPS = 1e-5
RMS_EPS = 1e-6

LANE = 128
KCAT = MLA_KV_RANK + LANE
PIN_PAD = 2048
NEG = -1e30
VMEM_LIMIT = 56 * 1024 * 1024

NN = (((1,), (0,)), ((), ()))
NT = (((1,), (1,)), ((), ()))
TN = (((0,), (0,)), ((), ()))


def _tile(n, pref):
    t = min(n, pref)
    while t > 8 and (n % t or t % 8):
        t -= 8
    assert n % t == 0, (n, pref)
    return t


def _cparams(sem):
    return pltpu.CompilerParams(dimension_semantics=sem, vmem_limit_bytes=VMEM_LIMIT)


def _full(shape):
    nd = len(shape)
    return pl.BlockSpec(shape, lambda *_: (0,) * nd)


def _split(x):
    hi = x.astype(BF16)
    lo = (x - hi.astype(F32)).astype(BF16)
    return hi, lo


def _dot3(a, b, dims=NN):
    ah, al = _split(a)
    bh, bl = _split(b)
    d = lambda p, q: lax.dot_general(p, q, dims, preferred_element_type=F32)
    return d(ah, bh) + d(ah, bl) + d(al, bh)


def _dotb(a, b, dims=NN):
    return lax.dot_general(a.astype(BF16), b.astype(BF16), dims, preferred_element_type=F32)


def _sigmoid(x):
    return 1.0 / (1.0 + jnp.exp(-x))


def _layer_norm(x, g, b):
    mu = jnp.mean(x, axis=-1, keepdims=True)
    xc = x - mu
    var = jnp.mean(xc * xc, axis=-1, keepdims=True)
    return xc * lax.rsqrt(var + LN_EPS) * g + b


def _proj_kernel(x_ref, win_ref, cos_ref, sin_ref, qg_ref, kvg_ref, wq_ref, wuk_ref,
                 pa_ref, pb_ref, ckv_ref, kr_ref, kcat_ref, qcat_ref):
    p = _dotb(x_ref[...], win_ref[...])
    pa_ref[...] = p[:, 0:256]
    pb_ref[...] = p[:, 1152:1152 + RWKV_COLS]
    qc = p[:, 256:640]
    kvc = p[:, 640:896]
    cos = cos_ref[...]
    sin = sin_ref[...]
    kr = p[:, 896:1024] * cos + p[:, 1024:1152] * sin
    ckv = kvc * lax.rsqrt(jnp.mean(kvc * kvc, axis=-1, keepdims=True) + RMS_EPS) * kvg_ref[...]
    ckv_ref[...] = ckv
    kr_ref[...] = kr[:, :MLA_ROPE]
    kcat_ref[...] = jnp.concatenate([ckv, kr], axis=-1).astype(BF16)
    qn = qc * lax.rsqrt(jnp.mean(qc * qc, axis=-1, keepdims=True) + RMS_EPS) * qg_ref[...]
    q = _dotb(qn, wq_ref[...])
    for h in range(MLA_HEADS):
        ql = _dotb(q[:, h * LANE:(h + 1) * LANE], wuk_ref[h])
        qr = (q[:, 1024 + h * LANE:1024 + (h + 1) * LANE] * cos
              + q[:, 2048 + h * LANE:2048 + (h + 1) * LANE] * sin)
        qcat_ref[h] = (jnp.concatenate([ql, qr], axis=-1) * QK_SCALE).astype(BF16)


def _proj(x, cos, sin, w):
    n = x.shape[0]
    t = _tile(n, 256)
    row = lambda c: pl.BlockSpec((t, c), lambda i: (i, 0))
    return pl.pallas_call(
        _proj_kernel,
        grid=(n // t,),
        in_specs=[row(D_MODEL), _full((D_MODEL, PIN_PAD)), row(LANE), row(LANE),
                  _full((1, MLA_Q_RANK)), _full((1, MLA_KV_RANK)),
                  _full((MLA_Q_RANK, 3 * MLA_HEADS * LANE)), _full((MLA_HEADS, LANE, MLA_KV_RANK))],
        out_specs=[row(POOL_WIDTH), row(RWKV_COLS), row(MLA_KV_RANK), row(MLA_ROPE), row(KCAT),
                   pl.BlockSpec((MLA_HEADS, t, KCAT), lambda i: (0, i, 0))],
        out_shape=[jax.ShapeDtypeStruct((n, POOL_WIDTH), F32), jax.ShapeDtypeStruct((n, RWKV_COLS), F32),
                   jax.ShapeDtypeStruct((n, MLA_KV_RANK), F32), jax.ShapeDtypeStruct((n, MLA_ROPE), F32),
                   jax.ShapeDtypeStruct((n, KCAT), BF16), jax.ShapeDtypeStruct((MLA_HEADS, n, KCAT), BF16)],
        compiler_params=_cparams(("parallel",)),
        name="proj",
    )(x, w["win"], cos, sin, w["qg"], w["kvg"], w["wq"], w["wuk"])


def _pool_kernel(p_ref, w_ref, sc_ref, o_ref, e_ref, c2_ref, c4_ref, c8_ref, *, t, full_count):
    i = pl.program_id(1)
    n = t + 16

    @pl.when(i == 0)
    def _():
        e_ref[0:16, :] = jnp.zeros((16, POOL_WIDTH), F32)

    @pl.when(i > 0)
    def _():
        e_ref[0:16, :] = e_ref[t:n, :]

    p = p_ref[...]
    e_ref[16:n, :] = p
    c2_ref[1:n, :] = e_ref[1:n, :] + e_ref[0:n - 1, :]
    c4_ref[3:n, :] = c2_ref[3:n, :] + c2_ref[1:n - 2, :]
    c8_ref[7:n, :] = c4_ref[7:n, :] + c4_ref[3:n - 4, :]
    c16 = c8_ref[16:n, :] + c8_ref[8:n - 8, :]
    lane = lax.broadcasted_iota(jnp.int32, (t, POOL_WIDTH), 1)
    win = jnp.where(lane < 64, c2_ref[16:n, :],
                    jnp.where(lane < 128, c4_ref[16:n, :], jnp.where(lane < 192, c8_ref[16:n, :], c16)))
    wlen = jnp.where(lane < 64, 2, jnp.where(lane < 128, 4, jnp.where(lane < 192, 8, 16)))
    if full_count:
        cnt = wlen
    else:
        pos = i * t + lax.broadcasted_iota(jnp.int32, (t, POOL_WIDTH), 0)
        cnt = jnp.minimum(pos + 1, wlen)
    d = win / cnt.astype(F32) - p
    o_ref[...] = (_dotb(d, w_ref[...]) * sc_ref[...]).astype(BF16)


def _pool(pa, nseq, w, full_count):
    n = pa.shape[0]
    s = n // nseq
    t = _tile(s, 512)
    nt = s // t
    buf = pltpu.VMEM((t + 16, POOL_WIDTH), F32)
    return pl.pallas_call(
        functools.partial(_pool_kernel, t=t, full_count=full_count),
        grid=(nseq, nt),
        in_specs=[pl.BlockSpec((t, POOL_WIDTH), lambda b, i: (b * nt + i, 0)),
                  _full((POOL_WIDTH, POOL_WIDTH)), _full((1, POOL_WIDTH))],
        out_specs=pl.BlockSpec((t, POOL_WIDTH), lambda b, i: (b * nt + i, 0)),
        out_shape=jax.ShapeDtypeStruct((n, POOL_WIDTH), BF16),
        scratch_shapes=[buf, buf, buf, buf],
        compiler_params=_cparams(("arbitrary", "arbitrary")),
        name="pool",
    )(pa, w["pool_w"], w["pool_scale"])


def _rwkv_pre_kernel(*refs, t, has_vres):
    if has_vres:
        (pb_ref, mu_ref, wl_ref, w0_ref, a0_ref, kkp_ref, ka_ref, bd_ref, vf_ref, v0_ref, v1_ref, v2_ref,
         r_ref, lw_ref, k_ref, v_ref, kk_ref, a_ref, g_ref, e_ref) = refs
    else:
        (pb_ref, mu_ref, wl_ref, w0_ref, a0_ref, kkp_ref, ka_ref, bd_ref,
         r_ref, lw_ref, k_ref, v_ref, kk_ref, a_ref, g_ref, e_ref) = refs
    i = pl.program_id(1)

    @pl.when(i == 0)
    def _():
        e_ref[0:8, :] = jnp.zeros((8, RWKV_COLS), F32)

    @pl.when(i > 0)
    def _():
        e_ref[0:8, :] = e_ref[t:t + 8, :]

    pb = pb_ref[...]
    e_ref[8:t + 8, :] = pb
    prev = e_ref[7:t + 7, :]
    xs = pb + (prev - pb) * mu_ref[...]
    r = xs[:, 0:256]
    k = xs[:, 256:512]
    v = xs[:, 512:768]
    lo = xs[:, 768:832]
    lane = lax.broadcasted_iota(jnp.int32, lo.shape, 1)
    act = jnp.where(lane < 16, jnp.tanh(lo), jnp.where(lane < 32, lo, _sigmoid(lo)))
    lora = _dot3(act, wl_ref[...])
    z = -(w0_ref[...] + lora[:, 0:256])
    softplus = jnp.maximum(z, 0.0) + jnp.log(1.0 + jnp.exp(-jnp.abs(z)))
    lw_ref[...] = -jnp.exp(-softplus - 0.5)
    if has_vres:
        vf = vf_ref[...]
        mix = _dot3(_dot3(v, v1_ref[...]), v2_ref[...])
        v = v + (vf - v) * _sigmoid(v0_ref[...] + mix)
    a = _sigmoid(a0_ref[...] + lora[:, 256:512])
    kk = k * kkp_ref[...]
    ss = _dot3(kk * kk, bd_ref[...])
    kk = kk / jnp.maximum(jnp.sqrt(ss), 1e-12)
    r_ref[...] = r
    k_ref[...] = k * (1.0 + (a - 1.0) * ka_ref[...])
    v_ref[...] = v
    kk_ref[...] = kk
    a_ref[...] = a
    g_ref[...] = lora[:, 512:768]


def _rwkv_pre(pb, nseq, w, vfirst):
    n = pb.shape[0]
    s = n // nseq
    t = _tile(s, 256)
    nt = s // t
    has_vres = vfirst is not None
    row = lambda c: pl.BlockSpec((t, c), lambda b, i: (b * nt + i, 0))
    vec = _full((1, RWKV_WIDTH))
    in_specs = [row(RWKV_COLS), _full((1, RWKV_COLS)), _full((64, 768)), vec, vec, vec, vec,
                _full((RWKV_WIDTH, RWKV_WIDTH))]
    args = [pb, w["mu"], w["lora"], w["w0"], w["a0"], w["kkp"], w["ka"], w["bd"]]
    if has_vres:
        in_specs += [row(RWKV_WIDTH), vec, _full((RWKV_WIDTH, LANE)), _full((LANE, RWKV_WIDTH))]
        args += [vfirst, w["v0"], w["v1"], w["v2"]]
    return pl.pallas_call(
        functools.partial(_rwkv_pre_kernel, t=t, has_vres=has_vres),
        grid=(nseq, nt),
        in_specs=in_specs,
        out_specs=[row(RWKV_WIDTH)] * 7,
        out_shape=[jax.ShapeDtypeStruct((n, RWKV_WIDTH), F32)] * 7,
        scratch_shapes=[pltpu.VMEM((t + 8, RWKV_COLS), F32)],
        compiler_params=_cparams(("arbitrary", "arbitrary")),
        name="rwkv_pre",
    )(*args)


SCAN_BLOCK = 8


def _scan_chunk(r, lw, k, v, kk, a, st, c):
    hc = RWKV_HEADS * c
    lane = lax.broadcasted_iota(jnp.int32, (c, RWKV_WIDTH), 1)
    ri = lax.broadcasted_iota(jnp.int32, (c, c), 0)
    ci = lax.broadcasted_iota(jnp.int32, (c, c), 1)
    tril = jnp.where(ri >= ci, 1.0, 0.0).astype(F32)
    cum = jnp.dot(tril, lw, preferred_element_type=F32, precision=lax.Precision.HIGHEST)
    tot = cum[c - 1:c, :]
    e_neg = jnp.exp(-cum)
    at = -kk * jnp.exp(cum - lw)
    bt = kk * a * e_neg
    kt = k * e_neg
    rt = r * jnp.exp(cum)
    e_rem = jnp.exp(tot - cum)
    bh = kk * a * e_rem
    kh = k * e_rem

    def bd(x):
        return jnp.concatenate([jnp.where((lane >= h * RWKV_HEAD) & (lane < (h + 1) * RWKV_HEAD), x, 0.0)
                                for h in range(RWKV_HEADS)], axis=0)

    lst = jnp.concatenate([bd(at), bd(rt)], axis=0)
    rst = jnp.concatenate([bd(bt), bd(kt)], axis=0)
    g = _dot3(lst, rst, NT)
    gr = lax.broadcasted_iota(jnp.int32, (hc, hc), 0) % c
    gc = lax.broadcasted_iota(jnp.int32, (hc, hc), 1) % c
    strict = gc < gr
    incl = gc <= gr
    n_ab = jnp.where(strict, g[0:hc, 0:hc], 0.0)
    a_ak = jnp.where(strict, g[0:hc, hc:2 * hc], 0.0)
    a_rb = jnp.where(incl, g[hc:2 * hc, 0:hc], 0.0)
    a_rk = jnp.where(incl, g[hc:2 * hc, hc:2 * hc], 0.0)
    eye = (lax.broadcasted_iota(jnp.int32, (hc, hc), 0) == lax.broadcasted_iota(jnp.int32, (hc, hc), 1))
    ident = jnp.where(eye, 1.0, 0.0)

    def neumann(x, order, dot):
        out = ident + x
        for _ in range(int(math.log2(order)) - 1):
            x = dot(x, x)
            out = out + dot(out, x)
        return out

    blk = min(SCAN_BLOCK, c)
    diag_blocks = jnp.where(gr // blk == gc // blk, n_ab, 0.0)
    inv = neumann(diag_blocks, blk, _dot3)
    if blk < c:
        inv = _dot3(neumann(_dot3(inv, n_ab - diag_blocks), c // blk, _dotb), inv)
    w12 = _dot3(lst, st)
    vr = jnp.concatenate([v[:, h * RWKV_HEAD:(h + 1) * RWKV_HEAD] for h in range(RWKV_HEADS)], axis=0)
    u = _dot3(inv, w12[0:hc] + _dot3(a_ak, vr))
    yr = w12[hc:2 * hc] + _dot3(a_rb, u) + _dot3(a_rk, vr)
    y = jnp.concatenate([yr[h * c:(h + 1) * c, :] for h in range(RWKV_HEADS)], axis=1)
    e256r = lax.broadcasted_iota(jnp.int32, (RWKV_WIDTH, RWKV_WIDTH), 0)
    e256c = lax.broadcasted_iota(jnp.int32, (RWKV_WIDTH, RWKV_WIDTH), 1)
    dec = jnp.where(e256r == e256c, jnp.broadcast_to(jnp.exp(tot), (RWKV_WIDTH, RWKV_WIDTH)), 0.0)
    lhs = jnp.concatenate([dec, bd(bh), bd(kh)], axis=0)
    rhs = jnp.concatenate([st, u, vr], axis=0)
    return y, _dot3(lhs, rhs, TN)


def _scan_kernel(r_ref, lw_ref, k_ref, v_ref, kk_ref, a_ref, s0_ref, y_ref, sf_ref, st_ref, *, c, nch, nb):
    j = pl.program_id(1)

    @pl.when(j == 0)
    def _():
        st_ref[...] = s0_ref[...]

    def body(ch, carry):
        sl = pl.ds(pl.multiple_of(ch * c, c), c)
        for i in range(nb):
            y, st = _scan_chunk(r_ref[i, sl, :], lw_ref[i, sl, :], k_ref[i, sl, :], v_ref[i, sl, :],
                                kk_ref[i, sl, :], a_ref[i, sl, :], st_ref[i], c)
            y_ref[i, sl, :] = y
            st_ref[i] = st
        return carry

    lax.fori_loop(0, nch, body, 0)

    @pl.when(j == pl.num_programs(1) - 1)
    def _():
        sf_ref[...] = st_ref[...]


def _scan(r, lw, k, v, kk, a, s0, c):
    n = r.shape[0]
    nseq = s0.shape[0]
    s = n // nseq
    nb = 2 if s > c else math.gcd(nseq, 8)
    nch = max(1, min(8, s // c))
    while (s // c) % nch:
        nch -= 1
    t = c * nch
    nt = s // t
    row = pl.BlockSpec((nb, t, RWKV_WIDTH), lambda b, i: (b, i, 0))
    st_spec = pl.BlockSpec((nb, RWKV_WIDTH, RWKV_HEAD), lambda b, i: (b, 0, 0))
    seqs = [x.reshape(nseq, s, RWKV_WIDTH) for x in (r, lw, k, v, kk, a)]
    y, sfin = pl.pallas_call(
        functools.partial(_scan_kernel, c=c, nch=nch, nb=nb),
        grid=(nseq // nb, nt),
        in_specs=[row] * 6 + [st_spec],
        out_specs=[row, st_spec],
        out_shape=[jax.ShapeDtypeStruct((nseq, s, RWKV_WIDTH), F32),
                   jax.ShapeDtypeStruct((nseq, RWKV_WIDTH, RWKV_HEAD), F32)],
        scratch_shapes=[pltpu.VMEM((nb, RWKV_WIDTH, RWKV_HEAD), F32)],
        compiler_params=_cparams(("arbitrary", "arbitrary")),
        name="wkv_scan",
    )(*seqs, s0)
    return y.reshape(n, RWKV_WIDTH), sfin


def _latent_to_heads(o, wuv_ref, tq):
    out = None
    for h in range(MLA_HEADS):
        part = _dotb(o[h * tq:(h + 1) * tq, :], wuv_ref[h])
        out = part if out is None else out + part
    return out


CHAIN_ROWS = 1024


def _attn_prompt_kernel(qi_ref, ki_ref, q_ref, k_ref, wuv_ref, o_ref, m_ref, l_ref, acc_ref, *, tq, tk):
    step = pl.program_id(1)
    qi = qi_ref[step]
    ki = ki_ref[step]
    rows = MLA_HEADS * tq

    @pl.when(ki == 0)
    def _():
        m_ref[...] = jnp.full((rows, 1), NEG, F32)
        l_ref[...] = jnp.zeros((rows, LANE), F32)
        acc_ref[...] = jnp.zeros((rows, MLA_KV_RANK), F32)

    last = (qi * tq + tq - 1) // tk
    cheads = max(1, CHAIN_ROWS // tq)
    crows = cheads * tq

    def update(masked):
        k = k_ref[...]
        kv = k[:, :MLA_KV_RANK]
        if masked:
            rel = (lax.broadcasted_iota(jnp.int32, (crows, tk), 1)
                   - lax.broadcasted_iota(jnp.int32, (crows, tk), 0) % tq)
            visible = rel <= qi * tq - ki * tk
        for c in range(MLA_HEADS // cheads):
            r0 = c * crows
            q = q_ref[c * cheads:(c + 1) * cheads].reshape(crows, KCAT)
            s = lax.dot_general(q, k, NT, preferred_element_type=F32)
            if masked:
                s = jnp.where(visible, s, NEG)
            m_old = m_ref[r0:r0 + crows, :]
            m_new = jnp.maximum(m_old, jnp.max(s, axis=-1, keepdims=True))
            alpha = jnp.exp2(m_old - m_new)
            p = jnp.exp2(s - m_new)
            psum = p[:, 0:LANE]
            for t in range(1, tk // LANE):
                psum = psum + p[:, t * LANE:(t + 1) * LANE]
            l_ref[r0:r0 + crows, :] = alpha * l_ref[r0:r0 + crows, :] + psum
            acc_ref[r0:r0 + crows, :] = alpha * acc_ref[r0:r0 + crows, :] + jnp.dot(
                p.astype(BF16), kv, preferred_element_type=F32)
            m_ref[r0:r0 + crows, :] = m_new

    @pl.when(ki < last)
    def _():
        update(False)

    @pl.when(ki == last)
    def _():
        update(True)
        o = acc_ref[...] / jnp.sum(l_ref[...], axis=-1, keepdims=True)
        o_ref[...] = _latent_to_heads(o, wuv_ref, tq).astype(BF16)


def _attn_prompt(qcat, kcat, wuv, nseq):
    n = kcat.shape[0]
    s = n // nseq
    tq = _tile(s, 512)
    tk = _tile(s, 1024)
    nq, nk = s // tq, s // tk
    assert tk % LANE == 0 and tk % tq == 0
    pairs = [(qi, ki) for qi in range(nq) for ki in range((qi * tq + tq - 1) // tk + 1)]
    qi_tab = jnp.asarray(np.array([p[0] for p in pairs], np.int32))
    ki_tab = jnp.asarray(np.array([p[1] for p in pairs], np.int32))
    rows = MLA_HEADS * tq
    grid_spec = pltpu.PrefetchScalarGridSpec(
        num_scalar_prefetch=2,
        grid=(nseq, len(pairs)),
        in_specs=[pl.BlockSpec((MLA_HEADS, tq, KCAT), lambda b, p, qt, kt: (0, b * nq + qt[p], 0)),
                  pl.BlockSpec((tk, KCAT), lambda b, p, qt, kt: (b * nk + kt[p], 0)),
                  pl.BlockSpec((MLA_HEADS, MLA_KV_RANK, MLA_HEADS * MLA_V_DIM), lambda b, p, qt, kt: (0, 0, 0))],
        out_specs=pl.BlockSpec((tq, MLA_HEADS * MLA_V_DIM), lambda b, p, qt, kt: (b * nq + qt[p], 0)),
        scratch_shapes=[pltpu.VMEM((rows, 1), F32), pltpu.VMEM((rows, LANE), F32),
                        pltpu.VMEM((rows, MLA_KV_RANK), F32)],
    )
    return pl.pallas_call(
        functools.partial(_attn_prompt_kernel, tq=tq, tk=tk),
        grid_spec=grid_spec,
        out_shape=jax.ShapeDtypeStruct((n, MLA_HEADS * MLA_V_DIM), BF16),
        compiler_params=_cparams(("parallel", "arbitrary")),
        name="attn_prompt",
    )(qi_tab, ki_tab, qcat, kcat, wuv)


def _attn_sample_kernel(pt_ref, q_ref, kn_ref, wuv_ref, ckv_hbm, krt_hbm, o_ref, cbuf, rbuf, sems,
                        *, layer, ppc, nchunk, sd):
    b = pl.program_id(0)
    nb = pl.num_programs(0)
    rows = MLA_HEADS * sd

    def copies(bb, chunk, slot):
        out = []
        for i in range(ppc):
            page = pt_ref[bb, chunk * ppc + i]
            out.append(pltpu.make_async_copy(ckv_hbm.at[layer, page], cbuf.at[slot, i], sems.at[0, slot]))
            out.append(pltpu.make_async_copy(krt_hbm.at[layer, page],
                                             rbuf.at[slot, :, pl.ds(i * PAGE_SIZE, PAGE_SIZE)], sems.at[1, slot]))
        return out

    @pl.when(b == 0)
    def _():
        for cp in copies(0, 0, 0):
            cp.start()

    q = q_ref[...].reshape(rows, KCAT)
    ql = q[:, :MLA_KV_RANK]
    qr = q[:, MLA_KV_RANK:MLA_KV_RANK + MLA_ROPE]

    def accumulate(state, s, values):
        m, l, acc = state
        m_new = jnp.maximum(m, jnp.max(s, axis=-1, keepdims=True))
        alpha = jnp.exp2(m - m_new)
        p = jnp.exp2(s - m_new)
        l = alpha * l + jnp.sum(p, axis=-1, keepdims=True)
        acc = alpha * acc + jnp.dot(p.astype(BF16), values, preferred_element_type=F32)
        return m_new, l, acc

    state = (jnp.full((rows, 1), NEG, F32), jnp.zeros((rows, 1), F32), jnp.zeros((rows, MLA_KV_RANK), F32))
    for c in range(nchunk):
        slot = c % 2
        if c + 1 < nchunk:
            for cp in copies(b, c + 1, 1 - slot):
                cp.start()
        else:
            @pl.when(b + 1 < nb)
            def _():
                for cp in copies(b + 1, 0, 1 - slot):
                    cp.start()
        for cp in copies(b, c, slot):
            cp.wait()
        kc = cbuf[slot].reshape(ppc * PAGE_SIZE, MLA_KV_RANK).astype(BF16)
        krt = rbuf[slot].astype(BF16)
        s = (lax.dot_general(ql, kc, NT, preferred_element_type=F32)
             + jnp.dot(qr, krt, preferred_element_type=F32))
        state = accumulate(state, s, kc)
    kn = kn_ref[...]
    s = lax.dot_general(q, kn, NT, preferred_element_type=F32)
    qpos = lax.broadcasted_iota(jnp.int32, (rows, sd), 0) % sd
    kpos = lax.broadcasted_iota(jnp.int32, (rows, sd), 1)
    s = jnp.where(kpos <= qpos, s, NEG)
    _, l, acc = accumulate(state, s, kn[:, :MLA_KV_RANK])
    o_ref[...] = _latent_to_heads(acc / l, wuv_ref, sd).astype(BF16)


def _attn_sample(qcat, kcat, wuv, cache_ckv, cache_krope_t, page_table, layer):
    nb, n_pages = page_table.shape
    n = kcat.shape[0]
    sd = n // nb
    ppc = max(1, min(16, n_pages // 2))
    nchunk = n_pages // ppc
    assert n_pages % ppc == 0 and nchunk % 2 == 0 and sd % 8 == 0 and sd & (sd - 1) == 0
    grid_spec = pltpu.PrefetchScalarGridSpec(
        num_scalar_prefetch=1,
        grid=(nb,),
        in_specs=[pl.BlockSpec((MLA_HEADS, sd, KCAT), lambda b, pt: (0, b, 0)),
                  pl.BlockSpec((sd, KCAT), lambda b, pt: (b, 0)),
                  pl.BlockSpec((MLA_HEADS, MLA_KV_RANK, MLA_HEADS * MLA_V_DIM), lambda b, pt: (0, 0, 0)),
                  pl.BlockSpec(memory_space=pl.ANY), pl.BlockSpec(memory_space=pl.ANY)],
        out_specs=pl.BlockSpec((sd, MLA_HEADS * MLA_V_DIM), lambda b, pt: (b, 0)),
        scratch_shapes=[pltpu.VMEM((2, ppc, PAGE_SIZE, MLA_KV_RANK), F32),
                        pltpu.VMEM((2, MLA_ROPE, ppc * PAGE_SIZE), F32),
                        pltpu.SemaphoreType.DMA((2, 2))],
    )
    return pl.pallas_call(
        functools.partial(_attn_sample_kernel, layer=layer, ppc=ppc, nchunk=nchunk, sd=sd),
        grid_spec=grid_spec,
        out_shape=jax.ShapeDtypeStruct((n, MLA_HEADS * MLA_V_DIM), BF16),
        compiler_params=_cparams(("arbitrary",)),
        name="attn_sample",
    )(page_table, qcat, kcat, wuv, cache_ckv, cache_krope_t)


def _post_kernel(y_ref, r_ref, k_ref, v_ref, g_ref, ya_ref, yc_ref, x_ref, rk_ref, lg_ref, lb_ref, bd_ref,
                 wo_ref, g1_ref, b1_ref, o_ref):
    y = y_ref[...]
    bd = bd_ref[...]
    inv = 1.0 / RWKV_HEAD
    mu = _dot3(y, bd) * inv
    yc = y - mu
    var = _dot3(yc * yc, bd) * inv
    yn = yc * lax.rsqrt(var + RWKV_GN_EPS) * lg_ref[...] + lb_ref[...]
    bonus = _dot3(r_ref[...] * k_ref[...] * rk_ref[...], bd) * v_ref[...]
    yb = (yn + bonus) * g_ref[...]
    h = (jnp.dot(ya_ref[...], wo_ref[0:256, :], preferred_element_type=F32)
         + _dotb(yb, wo_ref[256:512, :])
         + jnp.dot(yc_ref[...], wo_ref[512:1024, :], preferred_element_type=F32))
    o_ref[...] = _layer_norm(DEEPNORM_ALPHA * x_ref[...] + h, g1_ref[...], b1_ref[...])


def _post(y, r, k, v, g, ya, yc, x, w):
    n = x.shape[0]
    t = _tile(n, 256)
    row = lambda c: pl.BlockSpec((t, c), lambda i: (i, 0))
    vec = _full((1, RWKV_WIDTH))
    vecd = _full((1, D_MODEL))
    return pl.pallas_call(
        _post_kernel,
        grid=(n // t,),
        in_specs=[row(256)] * 6 + [row(512), row(D_MODEL), vec, vec, vec, _full((256, 256)),
                                   _full((D_MODEL, D_MODEL)), vecd, vecd],
        out_specs=row(D_MODEL),
        out_shape=jax.ShapeDtypeStruct((n, D_MODEL), F32),
        compiler_params=_cparams(("parallel",)),
        name="post",
    )(y, r, k, v, g, ya, yc, x, w["rk"], w["lnx_g"], w["lnx_b"], w["bd"], w["wo"], w["ln1_g"], w["ln1_b"])


def _ffn_kernel(x_ref, wg_ref, wu_ref, wd_ref, g2_ref, b2_ref, o_ref, acc_ref):
    f = pl.program_id(1)

    @pl.when(f == 0)
    def _():
        acc_ref[...] = jnp.zeros(acc_ref.shape, F32)

    xb = x_ref[...].astype(BF16)
    g = jnp.dot(xb, wg_ref[...], preferred_element_type=F32)
    u = jnp.dot(xb, wu_ref[...], preferred_element_type=F32)
    hmid = g * _sigmoid(g) * u
    acc_ref[...] += jnp.dot(hmid.astype(BF16), wd_ref[...], preferred_element_type=F32)

    @pl.when(f == pl.num_programs(1) - 1)
    def _():
        o_ref[...] = _layer_norm(DEEPNORM_ALPHA * x_ref[...] + acc_ref[...], g2_ref[...], b2_ref[...])


def _ffn(x, w):
    n = x.shape[0]
    dff = w["ffn_g"].shape[1]
    tm = _tile(n, 1024)
    tf = 256
    vecd = _full((1, D_MODEL))
    return pl.pallas_call(
        _ffn_kernel,
        grid=(n // tm, dff // tf),
        in_specs=[pl.BlockSpec((tm, D_MODEL), lambda i, f: (i, 0)),
                  pl.BlockSpec((D_MODEL, tf), lambda i, f: (0, f)),
                  pl.BlockSpec((D_MODEL, tf), lambda i, f: (0, f)),
                  pl.BlockSpec((tf, D_MODEL), lambda i, f: (f, 0)), vecd, vecd],
        out_specs=pl.BlockSpec((tm, D_MODEL), lambda i, f: (i, 0)),
        out_shape=jax.ShapeDtypeStruct((n, D_MODEL), F32),
        scratch_shapes=[pltpu.VMEM((tm, D_MODEL), F32)],
        compiler_params=_cparams(("parallel", "arbitrary")),
        name="ffn",
    )(x, w["ffn_g"], w["ffn_u"], w["ffn_d"], w["ln2_g"], w["ln2_b"])


MOE_TILE = 512
IDX_BLOCK = 1024
TOP_K = 2


def _router_kernel(x_ref, rt_ref, tri_ref, meta_ref, cnt_ref, run_ref):
    i = pl.program_id(0)

    @pl.when(i == 0)
    def _():
        run_ref[...] = jnp.zeros(run_ref.shape, F32)

    logits = jnp.dot(x_ref[...], rt_ref[...], preferred_element_type=F32, precision=lax.Precision.HIGHEST)
    lane = lax.broadcasted_iota(jnp.int32, logits.shape, 1).astype(F32)
    logits = jnp.where(lane < N_EXPERTS, logits, NEG)
    m1 = jnp.max(logits, axis=-1, keepdims=True)
    i1 = jnp.min(jnp.where(logits == m1, lane, float(LANE)), axis=-1, keepdims=True)
    rest = jnp.where(lane == i1, NEG, logits)
    m2 = jnp.max(rest, axis=-1, keepdims=True)
    i2 = jnp.min(jnp.where(rest == m2, lane, float(LANE)), axis=-1, keepdims=True)
    e2 = jnp.exp(m2 - m1)
    g1 = 1.0 / (1.0 + e2)
    g2 = e2 * g1
    onehot = jnp.where(lane == i1, 1.0, jnp.where(lane == i2, 1.0, 0.0))
    before = jnp.dot(tri_ref[...], onehot.astype(BF16), preferred_element_type=F32) + run_ref[...]
    r1 = jnp.sum(jnp.where(lane == i1, before, 0.0), axis=-1, keepdims=True)
    r2 = jnp.sum(jnp.where(lane == i2, before, 0.0), axis=-1, keepdims=True)
    run_ref[...] += jnp.sum(onehot, axis=0, keepdims=True)
    meta = jnp.zeros(logits.shape, F32)
    for col, val in enumerate((i1, i2, g1, g2, r1, r2)):
        meta = jnp.where(lane == col, val, meta)
    meta_ref[...] = meta
    cnt_ref[...] = run_ref[...]


def _gather_rows_kernel(src_ref, x_hbm, o_ref, buf, sem):
    def copy(r):
        tok = src_ref[r // LANE, r % LANE]
        return pltpu.make_async_copy(x_hbm.at[pl.ds(tok, 1)], buf.at[pl.ds(r, 1)], sem)

    def start(r, c):
        copy(r).start()
        return c

    def wait(r, c):
        copy(r).wait()
        return c

    lax.fori_loop(0, IDX_BLOCK, start, 0, unroll=8)
    lax.fori_loop(0, IDX_BLOCK, wait, 0, unroll=8)
    o_ref[...] = buf[...].astype(BF16)


def _experts_kernel(te_ref, nv_ref, x_ref, wg_ref, wu_ref, wd_ref, o_ref, acc_ref):
    i = pl.program_id(0)
    f = pl.program_id(1)
    last = pl.num_programs(1) - 1

    @pl.when(i < nv_ref[0])
    def _():
        @pl.when(f == 0)
        def _():
            acc_ref[...] = jnp.zeros(acc_ref.shape, F32)

        xb = x_ref[...]
        g = jnp.dot(xb, wg_ref[0], preferred_element_type=F32)
        u = jnp.dot(xb, wu_ref[0], preferred_element_type=F32)
        hmid = g * _sigmoid(g) * u
        acc_ref[...] += jnp.dot(hmid.astype(BF16), wd_ref[0], preferred_element_type=F32)

        @pl.when(f == last)
        def _():
            o_ref[...] = acc_ref[...]

    @pl.when((i >= nv_ref[0]) & (f == last))
    def _():
        o_ref[...] = jnp.zeros(o_ref.shape, F32)


def _combine_kernel(d1_ref, d2_ref, meta_ref, x_ref, g2_ref, b2_ref, ys_hbm, o_ref, buf1, buf2, sems):
    def copies(r):
        a = d1_ref[r // LANE, r % LANE]
        b = d2_ref[r // LANE, r % LANE]
        return (pltpu.make_async_copy(ys_hbm.at[pl.ds(a, 1)], buf1.at[pl.ds(r, 1)], sems.at[0]),
                pltpu.make_async_copy(ys_hbm.at[pl.ds(b, 1)], buf2.at[pl.ds(r, 1)], sems.at[1]))

    def start(r, c):
        for cp in copies(r):
            cp.start()
        return c

    def wait(r, c):
        for cp in copies(r):
            cp.wait()
        return c

    lax.fori_loop(0, IDX_BLOCK, start, 0, unroll=8)
    lax.fori_loop(0, IDX_BLOCK, wait, 0, unroll=8)
    meta = meta_ref[...]
    y = meta[:, 2:3] * buf1[...] + meta[:, 3:4] * buf2[...]
    o_ref[...] = _layer_norm(DEEPNORM_ALPHA * x_ref[...] + y, g2_ref[...], b2_ref[...])


def _moe(x, w):
    n = x.shape[0]
    ne, _, dff = w["moe_g"].shape
    assert n % IDX_BLOCK == 0 and ne == N_EXPERTS
    tr = IDX_BLOCK
    meta, cnt = pl.pallas_call(
        _router_kernel,
        grid=(n // tr,),
        in_specs=[pl.BlockSpec((tr, D_MODEL), lambda i: (i, 0)), _full((D_MODEL, LANE)), _full((tr, tr))],
        out_specs=[pl.BlockSpec((tr, LANE), lambda i: (i, 0)), _full((1, LANE))],
        out_shape=[jax.ShapeDtypeStruct((n, LANE), F32), jax.ShapeDtypeStruct((1, LANE), F32)],
        scratch_shapes=[pltpu.VMEM((1, LANE), F32)],
        compiler_params=_cparams(("arbitrary",)),
        name="moe_router",
    )(x, w["router"], w["tri"])

    i1 = meta[:, 0].astype(jnp.int32)
    i2 = meta[:, 1].astype(jnp.int32)
    counts = cnt[0, :ne].astype(jnp.int32)
    padded = (counts + MOE_TILE - 1) // MOE_TILE * MOE_TILE
    ends = jnp.cumsum(padded)
    starts = ends - padded
    d1 = starts[i1] + meta[:, 4].astype(jnp.int32)
    d2 = starts[i2] + meta[:, 5].astype(jnp.int32)
    a_pad = TOP_K * n + ne * MOE_TILE
    tok = jnp.arange(n, dtype=jnp.int32)
    src = jnp.zeros((a_pad,), jnp.int32).at[d1].set(tok).at[d2].set(tok)
    n_tiles = a_pad // MOE_TILE
    tile_expert = jnp.minimum(jnp.searchsorted(ends, jnp.arange(n_tiles, dtype=jnp.int32) * MOE_TILE,
                                               side="right"), ne - 1).astype(jnp.int32)
    n_valid = (ends[-1] // MOE_TILE).astype(jnp.int32).reshape(1)

    idx_spec = pl.BlockSpec((IDX_BLOCK // LANE, LANE), lambda i: (i, 0), memory_space=pltpu.SMEM)
    xs = pl.pallas_call(
        _gather_rows_kernel,
        grid=(a_pad // IDX_BLOCK,),
        in_specs=[idx_spec, pl.BlockSpec(memory_space=pl.ANY)],
        out_specs=pl.BlockSpec((IDX_BLOCK, D_MODEL), lambda i: (i, 0)),
        out_shape=jax.ShapeDtypeStruct((a_pad, D_MODEL), BF16),
        scratch_shapes=[pltpu.VMEM((IDX_BLOCK, D_MODEL), F32), pltpu.SemaphoreType.DMA(())],
        compiler_params=_cparams(("arbitrary",)),
        name="moe_gather",
    )(src.reshape(a_pad // LANE, LANE), x)

    tf = 896 if dff % 896 == 0 else 512
    ys = pl.pallas_call(
        _experts_kernel,
        grid_spec=pltpu.PrefetchScalarGridSpec(
            num_scalar_prefetch=2,
            grid=(n_tiles, dff // tf),
            in_specs=[pl.BlockSpec((MOE_TILE, D_MODEL), lambda i, f, te, nv: (i, 0)),
                      pl.BlockSpec((1, D_MODEL, tf), lambda i, f, te, nv: (te[i], 0, f)),
                      pl.BlockSpec((1, D_MODEL, tf), lambda i, f, te, nv: (te[i], 0, f)),
                      pl.BlockSpec((1, tf, D_MODEL), lambda i, f, te, nv: (te[i], f, 0))],
            out_specs=pl.BlockSpec((MOE_TILE, D_MODEL), lambda i, f, te, nv: (i, 0)),
            scratch_shapes=[pltpu.VMEM((MOE_TILE, D_MODEL), F32)],
        ),
        out_shape=jax.ShapeDtypeStruct((a_pad, D_MODEL), F32),
        compiler_params=_cparams(("arbitrary", "arbitrary")),
        name="moe_experts",
    )(tile_expert, n_valid, xs, w["moe_g"], w["moe_u"], w["moe_d"])

    vecd = _full((1, D_MODEL))
    row = lambda c: pl.BlockSpec((IDX_BLOCK, c), lambda i: (i, 0))
    return pl.pallas_call(
        _combine_kernel,
        grid=(n // IDX_BLOCK,),
        in_specs=[idx_spec, idx_spec, row(LANE), row(D_MODEL), vecd, vecd, pl.BlockSpec(memory_space=pl.ANY)],
        out_specs=row(D_MODEL),
        out_shape=jax.ShapeDtypeStruct((n, D_MODEL), F32),
        scratch_shapes=[pltpu.VMEM((IDX_BLOCK, D_MODEL), F32), pltpu.VMEM((IDX_BLOCK, D_MODEL), F32),
                        pltpu.SemaphoreType.DMA((2,))],
        compiler_params=_cparams(("arbitrary",)),
        name="moe_combine",
    )(d1.reshape(n // LANE, LANE), d2.reshape(n // LANE, LANE), meta, x, w["ln2_g"], w["ln2_b"], ys)


_RWKV_PERM = np.concatenate([np.arange(0, 256), np.arange(272, 528), np.arange(528, 784),
                             np.arange(256, 272), np.arange(784, 800), np.arange(800, 832)])
_RWKV_INV = np.argsort(_RWKV_PERM)


def _rot_cols(wr):
    half = MLA_ROPE // 2
    return jnp.concatenate([-wr[..., half:], wr[..., :half]], axis=-1)


def _pad_last(x, width):
    return jnp.pad(x, [(0, 0)] * (x.ndim - 1) + [(0, width - x.shape[-1])])


def _prep_layer(l, p):
    w = {}
    w_in = p["w_in"][l]
    rw = w_in[:, 256:256 + RWKV_COLS][:, _RWKV_PERM]
    mla = w_in[:, 256 + RWKV_COLS:]
    wkr = mla[:, MLA_Q_RANK + MLA_KV_RANK:]
    win = jnp.concatenate([w_in[:, :256], mla[:, :MLA_Q_RANK + MLA_KV_RANK], _pad_last(wkr, LANE),
                           _pad_last(_rot_cols(wkr), LANE), rw], axis=1)
    w["win"] = _pad_last(win, PIN_PAD).astype(BF16)
    wuq = p["mla_wuq"][l]
    wq_rope = wuq[:, :, MLA_NOPE:]
    w["wq"] = jnp.concatenate([_pad_last(wuq[:, :, :MLA_NOPE], LANE).reshape(MLA_Q_RANK, -1),
                               _pad_last(wq_rope, LANE).reshape(MLA_Q_RANK, -1),
                               _pad_last(_rot_cols(wq_rope), LANE).reshape(MLA_Q_RANK, -1)], axis=1).astype(BF16)
    wuk = jnp.transpose(p["mla_wuk"][l], (1, 2, 0))
    w["wuk"] = jnp.pad(wuk, ((0, 0), (0, LANE - MLA_NOPE), (0, 0))).astype(BF16)
    wuv = jnp.transpose(p["mla_wuv"][l], (1, 0, 2))
    eye = jnp.eye(MLA_HEADS, dtype=F32)
    w["wuv"] = (wuv[:, :, None, :] * eye[:, None, :, None]).reshape(MLA_HEADS, MLA_KV_RANK, -1).astype(BF16)
    w["qg"] = p["mla_qnorm_g"][l][None]
    w["kvg"] = p["mla_kvnorm_g"][l][None]
    pw = p["pool_w"][l]
    eye4 = jnp.eye(4, dtype=F32)
    w["pool_w"] = (pw[:, :, None, :] * eye4[:, None, :, None]).reshape(POOL_WIDTH, POOL_WIDTH).astype(BF16)
    w["pool_scale"] = p["pool_scale"][l][None]
    w["mu"] = p["rwkv_mu"][l][_RWKV_PERM][None]
    lora = jnp.zeros((64, 768), F32)
    lora = lora.at[0:16, 0:256].set(p["rwkv_w2"][l]).at[16:32, 256:512].set(p["rwkv_a2"][l])
    w["lora"] = lora.at[32:64, 512:768].set(p["rwkv_g2"][l])
    for name, key in (("w0", "rwkv_w0"), ("a0", "rwkv_a0"), ("kkp", "rwkv_kk"), ("ka", "rwkv_ka"),
                      ("lnx_g", "rwkv_lnx_g"), ("lnx_b", "rwkv_lnx_b")):
        w[name] = p[key][l][None]
    w["rk"] = p["rwkv_rk"][l].reshape(1, RWKV_WIDTH)
    w["bd"] = jnp.kron(jnp.eye(RWKV_HEADS, dtype=F32), jnp.ones((RWKV_HEAD, RWKV_HEAD), F32))
    if l > 0:
        w["v0"] = p["vres_v0"][l - 1][None]
        w["v1"] = _pad_last(p["vres_w1"][l - 1], LANE)
        w["v2"] = jnp.pad(p["vres_w2"][l - 1], ((0, LANE - p["vres_w2"].shape[1]), (0, 0)))
    w["wo"] = p["w_out"][l].astype(BF16)
    for name in ("ln1_g", "ln1_b", "ln2_g", "ln2_b"):
        w[name] = p[name][l][None]
    if l % 2 == 0:
        w["ffn_g"] = p["ffn_w_gate"][l // 2].astype(BF16)
        w["ffn_u"] = p["ffn_w_up"][l // 2].astype(BF16)
        w["ffn_d"] = p["ffn_w_down"][l // 2].astype(BF16)
    else:
        w["router"] = _pad_last(p["moe_router"][l // 2], LANE)
        tri = np.tril(np.ones((IDX_BLOCK, IDX_BLOCK), np.float32), -1)
        w["tri"] = jnp.asarray(tri, BF16)
        w["moe_g"] = p["moe_w_gate"][l // 2].astype(BF16)
        w["moe_u"] = p["moe_w_up"][l // 2].astype(BF16)
        w["moe_d"] = p["moe_w_down"][l // 2].astype(BF16)
    return w


def _rope_tables(pos):
    half = MLA_ROPE // 2
    freqs = ROPE_BASE ** (-jnp.arange(half, dtype=F32) / half)
    ang = pos.astype(F32)[:, None] * freqs[None, :]
    cos = jnp.cos(ang)
    sin = jnp.sin(ang)
    return (_pad_last(jnp.concatenate([cos, cos], axis=-1), LANE),
            _pad_last(jnp.concatenate([sin, sin], axis=-1), LANE))


def _state_to_kernel(s):
    return jnp.swapaxes(s, -1, -2).reshape(s.shape[0], RWKV_WIDTH, RWKV_HEAD)


def _state_from_kernel(s):
    return jnp.swapaxes(s.reshape(s.shape[0], RWKV_HEADS, RWKV_HEAD, RWKV_HEAD), -1, -2)


def _channel_mix(x_prompt, x_sample, l, w):
    if l % 2 == 0:
        return _ffn(x_prompt, w), _ffn(x_sample, w)
    n_prompt = x_prompt.shape[0]
    x = _moe(jnp.concatenate([x_prompt, x_sample], axis=0), w)
    return x[:n_prompt], x[n_prompt:]


def _run_prompt(x, layers):
    b, s, _ = x.shape
    xf = x.reshape(b * s, D_MODEL)
    cos, sin = _rope_tables(jnp.tile(jnp.arange(s, dtype=jnp.int32), b))
    vfirst = None
    outs = []
    for l, w in enumerate(layers):
        pa, pb, ckv, krope, kcat, qcat = _proj(xf, cos, sin, w)
        ya = _pool(pa, b, w, full_count=False)
        r, lw, k, v, kk, a, g = _rwkv_pre(pb, b, w, vfirst)
        if vfirst is None:
            vfirst = v
        y, sfin = _scan(r, lw, k, v, kk, a, jnp.zeros((b, RWKV_WIDTH, RWKV_HEAD), F32), c=_tile(s, 64))
        yc = _attn_prompt(qcat, kcat, w["wuv"], b)
        xf = yield _post(y, r, k, v, g, ya, yc, xf, w)
        outs.append((ckv.reshape(b, s, -1), krope.reshape(b, s, -1), pa.reshape(b, s, -1)[:, -POOL_BUF:],
                     pb.reshape(b, s, -1)[:, -1][:, _RWKV_INV], _state_from_kernel(sfin)))
    yield (xf.reshape(b, s, D_MODEL),) + tuple(jnp.stack([o[i] for o in outs]) for i in range(5))


def _run_sample(x, past_len, cache_ckv, cache_krope, state_pool, state_shift, state_wkv, page_table, layers):
    b, s, _ = x.shape
    xf = x.reshape(b * s, D_MODEL)
    cos, sin = _rope_tables(jnp.tile(past_len + jnp.arange(s, dtype=jnp.int32), b))
    pool_lead = -(POOL_BUF + s) % 8
    shift_lead = 7
    cache_krope_t = jnp.swapaxes(cache_krope, 2, 3)
    vfirst = None
    outs = []
    for l, w in enumerate(layers):
        pa, pb, ckv, krope, kcat, qcat = _proj(xf, cos, sin, w)
        ext = jnp.concatenate([state_pool[l], pa.reshape(b, s, -1)], axis=1)
        ext_p = jnp.pad(ext, ((0, 0), (pool_lead, 0), (0, 0)))
        ya = _pool(ext_p.reshape(-1, POOL_WIDTH), 1, w, full_count=True)
        ya = ya.reshape(b, -1, POOL_WIDTH)[:, -s:].reshape(b * s, POOL_WIDTH)
        pbe = jnp.concatenate([state_shift[l][:, None, _RWKV_PERM], pb.reshape(b, s, -1)], axis=1)
        pbe = jnp.pad(pbe, ((0, 0), (shift_lead, 0), (0, 0)))
        pre = _rwkv_pre(pbe.reshape(-1, RWKV_COLS), 1, w, vfirst)
        if vfirst is None:
            vfirst = pre[3]
        r, lw, k, v, kk, a, g = (t.reshape(b, -1, RWKV_WIDTH)[:, -s:].reshape(b * s, RWKV_WIDTH) for t in pre)
        y, sfin = _scan(r, lw, k, v, kk, a, _state_to_kernel(state_wkv[l]), c=s)
        yc = _attn_sample(qcat, kcat, w["wuv"], cache_ckv, cache_krope_t, page_table, l)
        xf = yield _post(y, r, k, v, g, ya, yc, xf, w)
        outs.append((ckv.reshape(b, s, -1), krope.reshape(b, s, -1), ext[:, -POOL_BUF:],
                     pb.reshape(b, s, -1)[:, -1][:, _RWKV_INV], _state_from_kernel(sfin)))
    yield (xf.reshape(b, s, D_MODEL),) + tuple(jnp.stack([o[i] for o in outs]) for i in range(5))


def kernel(x_prompt, x_sample, cache_ckv, cache_krope, state_pool, state_shift, state_wkv, page_table, ln1_g, ln1_b, ln2_g, ln2_b, w_in, pool_w, pool_scale, rwkv_mu, rwkv_w0, rwkv_w2, rwkv_a0, rwkv_a2, rwkv_g2, rwkv_kk, rwkv_ka, rwkv_rk, rwkv_lnx_g, rwkv_lnx_b, vres_v0, vres_w1, vres_w2, mla_qnorm_g, mla_wuq, mla_kvnorm_g, mla_wuk, mla_wuv, w_out, ffn_w_gate, ffn_w_up, ffn_w_down, moe_router, moe_w_gate, moe_w_up, moe_w_down):
    p = dict(ln1_g=ln1_g, ln1_b=ln1_b, ln2_g=ln2_g, ln2_b=ln2_b, w_in=w_in, pool_w=pool_w, pool_scale=pool_scale,
             rwkv_mu=rwkv_mu, rwkv_w0=rwkv_w0, rwkv_w2=rwkv_w2, rwkv_a0=rwkv_a0, rwkv_a2=rwkv_a2, rwkv_g2=rwkv_g2,
             rwkv_kk=rwkv_kk, rwkv_ka=rwkv_ka, rwkv_rk=rwkv_rk, rwkv_lnx_g=rwkv_lnx_g, rwkv_lnx_b=rwkv_lnx_b,
             vres_v0=vres_v0, vres_w1=vres_w1, vres_w2=vres_w2, mla_qnorm_g=mla_qnorm_g, mla_wuq=mla_wuq,
             mla_kvnorm_g=mla_kvnorm_g, mla_wuk=mla_wuk, mla_wuv=mla_wuv, w_out=w_out, ffn_w_gate=ffn_w_gate,
             ffn_w_up=ffn_w_up, ffn_w_down=ffn_w_down, moe_router=moe_router, moe_w_gate=moe_w_gate,
             moe_w_up=moe_w_up, moe_w_down=moe_w_down)
    layers = [_prep_layer(l, p) for l in range(DEPTH)]
    past_len = page_table.shape[1] * PAGE_SIZE
    prompt_trunk = _run_prompt(x_prompt, layers)
    sample_trunk = _run_sample(x_sample, past_len, cache_ckv, cache_krope, state_pool, state_shift, state_wkv,
                               page_table, layers)
    prompt, sample = next(prompt_trunk), next(sample_trunk)
    for l, w in enumerate(layers):
        x_p, x_s = _channel_mix(prompt, sample, l, w)
        prompt, sample = prompt_trunk.send(x_p), sample_trunk.send(x_s)
    return (prompt[0], sample[0]) + prompt[1:] + sample[1:]
```

```python
import functools
import math

import numpy as np
import jax
import jax.numpy as jnp
from jax import lax
from jax.experimental import pallas as pl
from jax.experimental.pallas import tpu as pltpu

F32 = jnp.float32
BF16 = jnp.bfloat16

D_MODEL = 1024
POOL_WIDTH = 256
POOL_WINDOWS = (2, 4, 8, 16)
POOL_GDIM = 64
POOL_BUF = 15
RWKV_HEAD = 64
RWKV_WIDTH = 256
RWKV_HEADS = 4
RWKV_COLS = 832
RWKV_GN_EPS = 64e-5
MLA_HEADS = 8
MLA_V_DIM = 64
MLA_NOPE = 64
MLA_ROPE = 32
MLA_Q_RANK = 384
MLA_KV_RANK = 256
MLA_SCALE = (MLA_NOPE + MLA_ROPE) ** -0.5
QK_SCALE = MLA_SCALE * math.log2(math.e)
ROPE_BASE = 10000.0
PAGE_SIZE = 128
N_EXPERTS = 8
DEPTH = 2
DEEPNORM_ALPHA = (2 * DEPTH) ** 0.25
LN_EPS = 1e-5
RMS_EPS = 1e-6

LANE = 128
KCAT = MLA_KV_RANK + LANE
PIN_PAD = 2048
NEG = -1e30
VMEM_LIMIT = 56 * 1024 * 1024

NN = (((1,), (0,)), ((), ()))
NT = (((1,), (1,)), ((), ()))
TN = (((0,), (0,)), ((), ()))


def _tile(n, pref):
    t = min(n, pref)
    while t > 8 and (n % t or t % 8):
        t -= 8
    assert n % t == 0, (n, pref)
    return t


def _cparams(sem):
    return pltpu.CompilerParams(dimension_semantics=sem, vmem_limit_bytes=VMEM_LIMIT)


def _full(shape):
    nd = len(shape)
    return pl.BlockSpec(shape, lambda *_: (0,) * nd)


def _split(x):
    hi = x.astype(BF16)
    lo = (x - hi.astype(F32)).astype(BF16)
    return hi, lo


def _dot3(a, b, dims=NN):
    ah, al = _split(a)
    bh, bl = _split(b)
    d = lambda p, q: lax.dot_general(p, q, dims, preferred_element_type=F32)
    return d(ah, bh) + d(ah, bl) + d(al, bh)


def _dotb(a, b, dims=NN):
    return lax.dot_general(a.astype(BF16), b.astype(BF16), dims, preferred_element_type=F32)


def _sigmoid(x):
    return 1.0 / (1.0 + jnp.exp(-x))


def _layer_norm(x, g, b):
    mu = jnp.mean(x, axis=-1, keepdims=True)
    xc = x - mu
    var = jnp.mean(xc * xc, axis=-1, keepdims=True)
    return xc * lax.rsqrt(var + LN_EPS) * g + b


def _proj_kernel(x_ref, win_ref, cos_ref, sin_ref, qg_ref, kvg_ref, wq_ref, wuk_ref,
                 pa_ref, pb_ref, ckv_ref, kr_ref, kcat_ref, qcat_ref):
    p = _dotb(x_ref[...], win_ref[...])
    pa_ref[...] = p[:, 0:256]
    pb_ref[...] = p[:, 1152:1152 + RWKV_COLS]
    qc = p[:, 256:640]
    kvc = p[:, 640:896]
    cos = cos_ref[...]
    sin = sin_ref[...]
    kr = p[:, 896:1024] * cos + p[:, 1024:1152] * sin
    ckv = kvc * lax.rsqrt(jnp.mean(kvc * kvc, axis=-1, keepdims=True) + RMS_EPS) * kvg_ref[...]
    ckv_ref[...] = ckv
    kr_ref[...] = kr[:, :MLA_ROPE]
    kcat_ref[...] = jnp.concatenate([ckv, kr], axis=-1).astype(BF16)
    qn = qc * lax.rsqrt(jnp.mean(qc * qc, axis=-1, keepdims=True) + RMS_EPS) * qg_ref[...]
    q = _dotb(qn, wq_ref[...])
    for h in range(MLA_HEADS):
        ql = _dotb(q[:, h * LANE:(h + 1) * LANE], wuk_ref[h])
        qr = (q[:, 1024 + h * LANE:1024 + (h + 1) * LANE] * cos
              + q[:, 2048 + h * LANE:2048 + (h + 1) * LANE] * sin)
        qcat_ref[h] = (jnp.concatenate([ql, qr], axis=-1) * QK_SCALE).astype(BF16)


def _proj(x, cos, sin, w):
    n = x.shape[0]
    t = _tile(n, 256)
    row = lambda c: pl.BlockSpec((t, c), lambda i: (i, 0))
    return pl.pallas_call(
        _proj_kernel,
        grid=(n // t,),
        in_specs=[row(D_MODEL), _full((D_MODEL, PIN_PAD)), row(LANE), row(LANE),
                  _full((1, MLA_Q_RANK)), _full((1, MLA_KV_RANK)),
                  _full((MLA_Q_RANK, 3 * MLA_HEADS * LANE)), _full((MLA_HEADS, LANE, MLA_KV_RANK))],
        out_specs=[row(POOL_WIDTH), row(RWKV_COLS), row(MLA_KV_RANK), row(MLA_ROPE), row(KCAT),
                   pl.BlockSpec((MLA_HEADS, t, KCAT), lambda i: (0, i, 0))],
        out_shape=[jax.ShapeDtypeStruct((n, POOL_WIDTH), F32), jax.ShapeDtypeStruct((n, RWKV_COLS), F32),
                   jax.ShapeDtypeStruct((n, MLA_KV_RANK), F32), jax.ShapeDtypeStruct((n, MLA_ROPE), F32),
                   jax.ShapeDtypeStruct((n, KCAT), BF16), jax.ShapeDtypeStruct((MLA_HEADS, n, KCAT), BF16)],
        compiler_params=_cparams(("parallel",)),
        name="proj",
    )(x, w["win"], cos, sin, w["qg"], w["kvg"], w["wq"], w["wuk"])


def _pool_kernel(p_ref, w_ref, sc_ref, o_ref, e_ref, c2_ref, c4_ref, c8_ref, *, t, full_count):
    i = pl.program_id(1)
    n = t + 16

    @pl.when(i == 0)
    def _():
        e_ref[0:16, :] = jnp.zeros((16, POOL_WIDTH), F32)

    @pl.when(i > 0)
    def _():
        e_ref[0:16, :] = e_ref[t:n, :]

    p = p_ref[...]
    e_ref[16:n, :] = p
    c2_ref[1:n, :] = e_ref[1:n, :] + e_ref[0:n - 1, :]
    c4_ref[3:n, :] = c2_ref[3:n, :] + c2_ref[1:n - 2, :]
    c8_ref[7:n, :] = c4_ref[7:n, :] + c4_ref[3:n - 4, :]
    c16 = c8_ref[16:n, :] + c8_ref[8:n - 8, :]
    lane = lax.broadcasted_iota(jnp.int32, (t, POOL_WIDTH), 1)
    win = jnp.where(lane < 64, c2_ref[16:n, :],
                    jnp.where(lane < 128, c4_ref[16:n, :], jnp.where(lane < 192, c8_ref[16:n, :], c16)))
    wlen = jnp.where(lane < 64, 2, jnp.where(lane < 128, 4, jnp.where(lane < 192, 8, 16)))
    if full_count:
        cnt = wlen
    else:
        pos = i * t + lax.broadcasted_iota(jnp.int32, (t, POOL_WIDTH), 0)
        cnt = jnp.minimum(pos + 1, wlen)
    d = win / cnt.astype(F32) - p
    o_ref[...] = (_dotb(d, w_ref[...]) * sc_ref[...]).astype(BF16)


def _pool(pa, nseq, w, full_count):
    n = pa.shape[0]
    s = n // nseq
    t = _tile(s, 512)
    nt = s // t
    buf = pltpu.VMEM((t + 16, POOL_WIDTH), F32)
    return pl.pallas_call(
        functools.partial(_pool_kernel, t=t, full_count=full_count),
        grid=(nseq, nt),
        in_specs=[pl.BlockSpec((t, POOL_WIDTH), lambda b, i: (b * nt + i, 0)),
                  _full((POOL_WIDTH, POOL_WIDTH)), _full((1, POOL_WIDTH))],
        out_specs=pl.BlockSpec((t, POOL_WIDTH), lambda b, i: (b * nt + i, 0)),
        out_shape=jax.ShapeDtypeStruct((n, POOL_WIDTH), BF16),
        scratch_shapes=[buf, buf, buf, buf],
        compiler_params=_cparams(("arbitrary", "arbitrary")),
        name="pool",
    )(pa, w["pool_w"], w["pool_scale"])


def _rwkv_pre_kernel(*refs, t, has_vres):
    if has_vres:
        (pb_ref, mu_ref, wl_ref, w0_ref, a0_ref, kkp_ref, ka_ref, bd_ref, vf_ref, v0_ref, v1_ref, v2_ref,
         r_ref, lw_ref, k_ref, v_ref, kk_ref, a_ref, g_ref, e_ref) = refs
    else:
        (pb_ref, mu_ref, wl_ref, w0_ref, a0_ref, kkp_ref, ka_ref, bd_ref,
         r_ref, lw_ref, k_ref, v_ref, kk_ref, a_ref, g_ref, e_ref) = refs
    i = pl.program_id(1)

    @pl.when(i == 0)
    def _():
        e_ref[0:8, :] = jnp.zeros((8, RWKV_COLS), F32)

    @pl.when(i > 0)
    def _():
        e_ref[0:8, :] = e_ref[t:t + 8, :]

    pb = pb_ref[...]
    e_ref[8:t + 8, :] = pb
    prev = e_ref[7:t + 7, :]
    xs = pb + (prev - pb) * mu_ref[...]
    r = xs[:, 0:256]
    k = xs[:, 256:512]
    v = xs[:, 512:768]
    lo = xs[:, 768:832]
    lane = lax.broadcasted_iota(jnp.int32, lo.shape, 1)
    act = jnp.where(lane < 16, jnp.tanh(lo), jnp.where(lane < 32, lo, _sigmoid(lo)))
    lora = _dot3(act, wl_ref[...])
    z = -(w0_ref[...] + lora[:, 0:256])
    softplus = jnp.maximum(z, 0.0) + jnp.log(1.0 + jnp.exp(-jnp.abs(z)))
    lw_ref[...] = -jnp.exp(-softplus - 0.5)
    if has_vres:
        vf = vf_ref[...]
        mix = _dot3(_dot3(v, v1_ref[...]), v2_ref[...])
        v = v + (vf - v) * _sigmoid(v0_ref[...] + mix)
    a = _sigmoid(a0_ref[...] + lora[:, 256:512])
    kk = k * kkp_ref[...]
    ss = _dot3(kk * kk, bd_ref[...])
    kk = kk / jnp.maximum(jnp.sqrt(ss), 1e-12)
    r_ref[...] = r
    k_ref[...] = k * (1.0 + (a - 1.0) * ka_ref[...])
    v_ref[...] = v
    kk_ref[...] = kk
    a_ref[...] = a
    g_ref[...] = lora[:, 512:768]


def _rwkv_pre(pb, nseq, w, vfirst):
    n = pb.shape[0]
    s = n // nseq
    t = _tile(s, 256)
    nt = s // t
    has_vres = vfirst is not None
    row = lambda c: pl.BlockSpec((t, c), lambda b, i: (b * nt + i, 0))
    vec = _full((1, RWKV_WIDTH))
    in_specs = [row(RWKV_COLS), _full((1, RWKV_COLS)), _full((64, 768)), vec, vec, vec, vec,
                _full((RWKV_WIDTH, RWKV_WIDTH))]
    args = [pb, w["mu"], w["lora"], w["w0"], w["a0"], w["kkp"], w["ka"], w["bd"]]
    if has_vres:
        in_specs += [row(RWKV_WIDTH), vec, _full((RWKV_WIDTH, LANE)), _full((LANE, RWKV_WIDTH))]
        args += [vfirst, w["v0"], w["v1"], w["v2"]]
    return pl.pallas_call(
        functools.partial(_rwkv_pre_kernel, t=t, has_vres=has_vres),
        grid=(nseq, nt),
        in_specs=in_specs,
        out_specs=[row(RWKV_WIDTH)] * 7,
        out_shape=[jax.ShapeDtypeStruct((n, RWKV_WIDTH), F32)] * 7,
        scratch_shapes=[pltpu.VMEM((t + 8, RWKV_COLS), F32)],
        compiler_params=_cparams(("arbitrary", "arbitrary")),
        name="rwkv_pre",
    )(*args)


SCAN_BLOCK = 8


def _scan_chunk(r, lw, k, v, kk, a, st, c):
    hc = RWKV_HEADS * c
    lane = lax.broadcasted_iota(jnp.int32, (c, RWKV_WIDTH), 1)
    ri = lax.broadcasted_iota(jnp.int32, (c, c), 0)
    ci = lax.broadcasted_iota(jnp.int32, (c, c), 1)
    tril = jnp.where(ri >= ci, 1.0, 0.0).astype(F32)
    cum = jnp.dot(tril, lw, preferred_element_type=F32, precision=lax.Precision.HIGHEST)
    tot = cum[c - 1:c, :]
    e_neg = jnp.exp(-cum)
    at = -kk * jnp.exp(cum - lw)
    bt = kk * a * e_neg
    kt = k * e_neg
    rt = r * jnp.exp(cum)
    e_rem = jnp.exp(tot - cum)
    bh = kk * a * e_rem
    kh = k * e_rem

    def bd(x):
        return jnp.concatenate([jnp.where((lane >= h * RWKV_HEAD) & (lane < (h + 1) * RWKV_HEAD), x, 0.0)
                                for h in range(RWKV_HEADS)], axis=0)

    lst = jnp.concatenate([bd(at), bd(rt)], axis=0)
    rst = jnp.concatenate([bd(bt), bd(kt)], axis=0)
    g = _dot3(lst, rst, NT)
    gr = lax.broadcasted_iota(jnp.int32, (hc, hc), 0) % c
    gc = lax.broadcasted_iota(jnp.int32, (hc, hc), 1) % c
    strict = gc < gr
    incl = gc <= gr
    n_ab = jnp.where(strict, g[0:hc, 0:hc], 0.0)
    a_ak = jnp.where(strict, g[0:hc, hc:2 * hc], 0.0)
    a_rb = jnp.where(incl, g[hc:2 * hc, 0:hc], 0.0)
    a_rk = jnp.where(incl, g[hc:2 * hc, hc:2 * hc], 0.0)
    eye = (lax.broadcasted_iota(jnp.int32, (hc, hc), 0) == lax.broadcasted_iota(jnp.int32, (hc, hc), 1))
    ident = jnp.where(eye, 1.0, 0.0)

    def neumann(x, order, dot):
        out = ident + x
        for _ in range(int(math.log2(order)) - 1):
            x = dot(x, x)
            out = out + dot(out, x)
        return out

    blk = min(SCAN_BLOCK, c)
    diag_blocks = jnp.where(gr // blk == gc // blk, n_ab, 0.0)
    inv = neumann(diag_blocks, blk, _dot3)
    if blk < c:
        inv = _dot3(neumann(_dot3(inv, n_ab - diag_blocks), c // blk, _dotb), inv)
    w12 = _dot3(lst, st)
    vr = jnp.concatenate([v[:, h * RWKV_HEAD:(h + 1) * RWKV_HEAD] for h in range(RWKV_HEADS)], axis=0)
    u = _dot3(inv, w12[0:hc] + _dot3(a_ak, vr))
    yr = w12[hc:2 * hc] + _dot3(a_rb, u) + _dot3(a_rk, vr)
    y = jnp.concatenate([yr[h * c:(h + 1) * c, :] for h in range(RWKV_HEADS)], axis=1)
    e256r = lax.broadcasted_iota(jnp.int32, (RWKV_WIDTH, RWKV_WIDTH), 0)
    e256c = lax.broadcasted_iota(jnp.int32, (RWKV_WIDTH, RWKV_WIDTH), 1)
    dec = jnp.where(e256r == e256c, jnp.broadcast_to(jnp.exp(tot), (RWKV_WIDTH, RWKV_WIDTH)), 0.0)
    lhs = jnp.concatenate([dec, bd(bh), bd(kh)], axis=0)
    rhs = jnp.concatenate([st, u, vr], axis=0)
    return y, _dot3(lhs, rhs, TN)


def _scan_kernel(r_ref, lw_ref, k_ref, v_ref, kk_ref, a_ref, s0_ref, y_ref, sf_ref, st_ref, *, c, nch, nb):
    j = pl.program_id(1)

    @pl.when(j == 0)
    def _():
        st_ref[...] = s0_ref[...]

    def body(ch, carry):
        sl = pl.ds(pl.multiple_of(ch * c, c), c)
        for i in range(nb):
            y, st = _scan_chunk(r_ref[i, sl, :], lw_ref[i, sl, :], k_ref[i, sl, :], v_ref[i, sl, :],
                                kk_ref[i, sl, :], a_ref[i, sl, :], st_ref[i], c)
            y_ref[i, sl, :] = y
            st_ref[i] = st
        return carry

    lax.fori_loop(0, nch, body, 0)

    @pl.when(j == pl.num_programs(1) - 1)
    def _():
        sf_ref[...] = st_ref[...]


def _scan(r, lw, k, v, kk, a, s0, c):
    n = r.shape[0]
    nseq = s0.shape[0]
    s = n // nseq
    nb = 2 if s > c else math.gcd(nseq, 8)
    nch = max(1, min(8, s // c))
    while (s // c) % nch:
        nch -= 1
    t = c * nch
    nt = s // t
    row = pl.BlockSpec((nb, t, RWKV_WIDTH), lambda b, i: (b, i, 0))
    st_spec = pl.BlockSpec((nb, RWKV_WIDTH, RWKV_HEAD), lambda b, i: (b, 0, 0))
    seqs = [x.reshape(nseq, s, RWKV_WIDTH) for x in (r, lw, k, v, kk, a)]
    y, sfin = pl.pallas_call(
        functools.partial(_scan_kernel, c=c, nch=nch, nb=nb),
        grid=(nseq // nb, nt),
        in_specs=[row] * 6 + [st_spec],
        out_specs=[row, st_spec],
        out_shape=[jax.ShapeDtypeStruct((nseq, s, RWKV_WIDTH), F32),
                   jax.ShapeDtypeStruct((nseq, RWKV_WIDTH, RWKV_HEAD), F32)],
        scratch_shapes=[pltpu.VMEM((nb, RWKV_WIDTH, RWKV_HEAD), F32)],
        compiler_params=_cparams(("arbitrary", "arbitrary")),
        name="wkv_scan",
    )(*seqs, s0)
    return y.reshape(n, RWKV_WIDTH), sfin


def _latent_to_heads(o, wuv_ref, tq):
    out = None
    for h in range(MLA_HEADS):
        part = _dotb(o[h * tq:(h + 1) * tq, :], wuv_ref[h])
        out = part if out is None else out + part
    return out


CHAIN_ROWS = 1024


def _attn_prompt_kernel(qi_ref, ki_ref, q_ref, k_ref, wuv_ref, o_ref, m_ref, l_ref, acc_ref, *, tq, tk):
    step = pl.program_id(1)
    qi = qi_ref[step]
    ki = ki_ref[step]
    rows = MLA_HEADS * tq

    @pl.when(ki == 0)
    def _():
        m_ref[...] = jnp.full((rows, 1), NEG, F32)
        l_ref[...] = jnp.zeros((rows, LANE), F32)
        acc_ref[...] = jnp.zeros((rows, MLA_KV_RANK), F32)

    last = (qi * tq + tq - 1) // tk
    cheads = max(1, CHAIN_ROWS // tq)
    crows = cheads * tq

    def update(masked):
        k = k_ref[...]
        kv = k[:, :MLA_KV_RANK]
        if masked:
            rel = (lax.broadcasted_iota(jnp.int32, (crows, tk), 1)
                   - lax.broadcasted_iota(jnp.int32, (crows, tk), 0) % tq)
            visible = rel <= qi * tq - ki * tk
        for c in range(MLA_HEADS // cheads):
            r0 = c * crows
            q = q_ref[c * cheads:(c + 1) * cheads].reshape(crows, KCAT)
            s = lax.dot_general(q, k, NT, preferred_element_type=F32)
            if masked:
                s = jnp.where(visible, s, NEG)
            m_old = m_ref[r0:r0 + crows, :]
            m_new = jnp.maximum(m_old, jnp.max(s, axis=-1, keepdims=True))
            alpha = jnp.exp2(m_old - m_new)
            p = jnp.exp2(s - m_new)
            psum = p[:, 0:LANE]
            for t in range(1, tk // LANE):
                psum = psum + p[:, t * LANE:(t + 1) * LANE]
            l_ref[r0:r0 + crows, :] = alpha * l_ref[r0:r0 + crows, :] + psum
            acc_ref[r0:r0 + crows, :] = alpha * acc_ref[r0:r0 + crows, :] + jnp.dot(
                p.astype(BF16), kv, preferred_element_type=F32)
            m_ref[r0:r0 + crows, :] = m_new

    @pl.when(ki < last)
    def _():
        update(False)

    @pl.when(ki == last)
    def _():
        update(True)
        o = acc_ref[...] / jnp.sum(l_ref[...], axis=-1, keepdims=True)
        o_ref[...] = _latent_to_heads(o, wuv_ref, tq).astype(BF16)


def _attn_prompt(qcat, kcat, wuv, nseq):
    n = kcat.shape[0]
    s = n // nseq
    tq = _tile(s, 512)
    tk = _tile(s, 1024)
    nq, nk = s // tq, s // tk
    assert tk % LANE == 0 and tk % tq == 0
    pairs = [(qi, ki) for qi in range(nq) for ki in range((qi * tq + tq - 1) // tk + 1)]
    qi_tab = jnp.asarray(np.array([p[0] for p in pairs], np.int32))
    ki_tab = jnp.asarray(np.array([p[1] for p in pairs], np.int32))
    rows = MLA_HEADS * tq
    grid_spec = pltpu.PrefetchScalarGridSpec(
        num_scalar_prefetch=2,
        grid=(nseq, len(pairs)),
        in_specs=[pl.BlockSpec((MLA_HEADS, tq, KCAT), lambda b, p, qt, kt: (0, b * nq + qt[p], 0)),
                  pl.BlockSpec((tk, KCAT), lambda b, p, qt, kt: (b * nk + kt[p], 0)),
                  pl.BlockSpec((MLA_HEADS, MLA_KV_RANK, MLA_HEADS * MLA_V_DIM), lambda b, p, qt, kt: (0, 0, 0))],
        out_specs=pl.BlockSpec((tq, MLA_HEADS * MLA_V_DIM), lambda b, p, qt, kt: (b * nq + qt[p], 0)),
        scratch_shapes=[pltpu.VMEM((rows, 1), F32), pltpu.VMEM((rows, LANE), F32),
                        pltpu.VMEM((rows, MLA_KV_RANK), F32)],
    )
    return pl.pallas_call(
        functools.partial(_attn_prompt_kernel, tq=tq, tk=tk),
        grid_spec=grid_spec,
        out_shape=jax.ShapeDtypeStruct((n, MLA_HEADS * MLA_V_DIM), BF16),
        compiler_params=_cparams(("parallel", "arbitrary")),
        name="attn_prompt",
    )(qi_tab, ki_tab, qcat, kcat, wuv)


def _attn_sample_kernel(pt_ref, q_ref, kn_ref, wuv_ref, ckv_hbm, krt_hbm, o_ref, cbuf, rbuf, sems,
                        *, layer, ppc, nchunk, sd):
    b = pl.program_id(0)
    nb = pl.num_programs(0)
    rows = MLA_HEADS * sd

    def copies(bb, chunk, slot):
        out = []
        for i in range(ppc):
            page = pt_ref[bb, chunk * ppc + i]
            out.append(pltpu.make_async_copy(ckv_hbm.at[layer, page], cbuf.at[slot, i], sems.at[0, slot]))
            out.append(pltpu.make_async_copy(krt_hbm.at[layer, page],
                                             rbuf.at[slot, :, pl.ds(i * PAGE_SIZE, PAGE_SIZE)], sems.at[1, slot]))
        return out

    @pl.when(b == 0)
    def _():
        for cp in copies(0, 0, 0):
            cp.start()

    q = q_ref[...].reshape(rows, KCAT)
    ql = q[:, :MLA_KV_RANK]
    qr = q[:, MLA_KV_RANK:MLA_KV_RANK + MLA_ROPE]

    def accumulate(state, s, values):
        m, l, acc = state
        m_new = jnp.maximum(m, jnp.max(s, axis=-1, keepdims=True))
        alpha = jnp.exp2(m - m_new)
        p = jnp.exp2(s - m_new)
        l = alpha * l + jnp.sum(p, axis=-1, keepdims=True)
        acc = alpha * acc + jnp.dot(p.astype(BF16), values, preferred_element_type=F32)
        return m_new, l, acc

    state = (jnp.full((rows, 1), NEG, F32), jnp.zeros((rows, 1), F32), jnp.zeros((rows, MLA_KV_RANK), F32))
    for c in range(nchunk):
        slot = c % 2
        if c + 1 < nchunk:
            for cp in copies(b, c + 1, 1 - slot):
                cp.start()
        else:
            @pl.when(b + 1 < nb)
            def _():
                for cp in copies(b + 1, 0, 1 - slot):
                    cp.start()
        for cp in copies(b, c, slot):
            cp.wait()
        kc = cbuf[slot].reshape(ppc * PAGE_SIZE, MLA_KV_RANK).astype(BF16)
        krt = rbuf[slot].astype(BF16)
        s = (lax.dot_general(ql, kc, NT, preferred_element_type=F32)
             + jnp.dot(qr, krt, preferred_element_type=F32))
        state = accumulate(state, s, kc)
    kn = kn_ref[...]
    s = lax.dot_general(q, kn, NT, preferred_element_type=F32)
    qpos = lax.broadcasted_iota(jnp.int32, (rows, sd), 0) % sd
    kpos = lax.broadcasted_iota(jnp.int32, (rows, sd), 1)
    s = jnp.where(kpos <= qpos, s, NEG)
    _, l, acc = accumulate(state, s, kn[:, :MLA_KV_RANK])
    o_ref[...] = _latent_to_heads(acc / l, wuv_ref, sd).astype(BF16)


def _attn_sample(qcat, kcat, wuv, cache_ckv, cache_krope_t, page_table, layer):
    nb, n_pages = page_table.shape
    n = kcat.shape[0]
    sd = n // nb
    ppc = max(1, min(16, n_pages // 2))
    nchunk = n_pages // ppc
    assert n_pages % ppc == 0 and nchunk % 2 == 0 and sd % 8 == 0 and sd & (sd - 1) == 0
    grid_spec = pltpu.PrefetchScalarGridSpec(
        num_scalar_prefetch=1,
        grid=(nb,),
        in_specs=[pl.BlockSpec((MLA_HEADS, sd, KCAT), lambda b, pt: (0, b, 0)),
                  pl.BlockSpec((sd, KCAT), lambda b, pt: (b, 0)),
                  pl.BlockSpec((MLA_HEADS, MLA_KV_RANK, MLA_HEADS * MLA_V_DIM), lambda b, pt: (0, 0, 0)),
                  pl.BlockSpec(memory_space=pl.ANY), pl.BlockSpec(memory_space=pl.ANY)],
        out_specs=pl.BlockSpec((sd, MLA_HEADS * MLA_V_DIM), lambda b, pt: (b, 0)),
        scratch_shapes=[pltpu.VMEM((2, ppc, PAGE_SIZE, MLA_KV_RANK), F32),
                        pltpu.VMEM((2, MLA_ROPE, ppc * PAGE_SIZE), F32),
                        pltpu.SemaphoreType.DMA((2, 2))],
    )
    return pl.pallas_call(
        functools.partial(_attn_sample_kernel, layer=layer, ppc=ppc, nchunk=nchunk, sd=sd),
        grid_spec=grid_spec,
        out_shape=jax.ShapeDtypeStruct((n, MLA_HEADS * MLA_V_DIM), BF16),
        compiler_params=_cparams(("arbitrary",)),
        name="attn_sample",
    )(page_table, qcat, kcat, wuv, cache_ckv, cache_krope_t)


def _post_kernel(y_ref, r_ref, k_ref, v_ref, g_ref, ya_ref, yc_ref, x_ref, rk_ref, lg_ref, lb_ref, bd_ref,
                 wo_ref, g1_ref, b1_ref, o_ref):
    y = y_ref[...]
    bd = bd_ref[...]
    inv = 1.0 / RWKV_HEAD
    mu = _dot3(y, bd) * inv
    yc = y - mu
    var = _dot3(yc * yc, bd) * inv
    yn = yc * lax.rsqrt(var + RWKV_GN_EPS) * lg_ref[...] + lb_ref[...]
    bonus = _dot3(r_ref[...] * k_ref[...] * rk_ref[...], bd) * v_ref[...]
    yb = (yn + bonus) * g_ref[...]
    h = (jnp.dot(ya_ref[...], wo_ref[0:256, :], preferred_element_type=F32)
         + _dotb(yb, wo_ref[256:512, :])
         + jnp.dot(yc_ref[...], wo_ref[512:1024, :], preferred_element_type=F32))
    o_ref[...] = _layer_norm(DEEPNORM_ALPHA * x_ref[...] + h, g1_ref[...], b1_ref[...])


def _post(y, r, k, v, g, ya, yc, x, w):
    n = x.shape[0]
    t = _tile(n, 256)
    row = lambda c: pl.BlockSpec((t, c), lambda i: (i, 0))
    vec = _full((1, RWKV_WIDTH))
    vecd = _full((1, D_MODEL))
    return pl.pallas_call(
        _post_kernel,
        grid=(n // t,),
        in_specs=[row(256)] * 6 + [row(512), row(D_MODEL), vec, vec, vec, _full((256, 256)),
                                   _full((D_MODEL, D_MODEL)), vecd, vecd],
        out_specs=row(D_MODEL),
        out_shape=jax.ShapeDtypeStruct((n, D_MODEL), F32),
        compiler_params=_cparams(("parallel",)),
        name="post",
    )(y, r, k, v, g, ya, yc, x, w["rk"], w["lnx_g"], w["lnx_b"], w["bd"], w["wo"], w["ln1_g"], w["ln1_b"])


def _ffn_kernel(x_ref, wg_ref, wu_ref, wd_ref, g2_ref, b2_ref, o_ref, acc_ref):
    f = pl.program_id(1)

    @pl.when(f == 0)
    def _():
        acc_ref[...] = jnp.zeros(acc_ref.shape, F32)

    xb = x_ref[...].astype(BF16)
    g = jnp.dot(xb, wg_ref[...], preferred_element_type=F32)
    u = jnp.dot(xb, wu_ref[...], preferred_element_type=F32)
    hmid = g * _sigmoid(g) * u
    acc_ref[...] += jnp.dot(hmid.astype(BF16), wd_ref[...], preferred_element_type=F32)

    @pl.when(f == pl.num_programs(1) - 1)
    def _():
        o_ref[...] = _layer_norm(DEEPNORM_ALPHA * x_ref[...] + acc_ref[...], g2_ref[...], b2_ref[...])


def _ffn(x, w):
    n = x.shape[0]
    dff = w["ffn_g"].shape[1]
    tm = _tile(n, 512)
    tf = 1408
    vecd = _full((1, D_MODEL))
    return pl.pallas_call(
        _ffn_kernel,
        grid=(n // tm, dff // tf),
        in_specs=[pl.BlockSpec((tm, D_MODEL), lambda i, f: (i, 0)),
                  pl.BlockSpec((D_MODEL, tf), lambda i, f: (0, f)),
                  pl.BlockSpec((D_MODEL, tf), lambda i, f: (0, f)),
                  pl.BlockSpec((tf, D_MODEL), lambda i, f: (f, 0)), vecd, vecd],
        out_specs=pl.BlockSpec((tm, D_MODEL), lambda i, f: (i, 0)),
        out_shape=jax.ShapeDtypeStruct((n, D_MODEL), F32),
        scratch_shapes=[pltpu.VMEM((tm, D_MODEL), F32)],
        compiler_params=_cparams(("parallel", "arbitrary")),
        name="ffn",
    )(x, w["ffn_g"], w["ffn_u"], w["ffn_d"], w["ln2_g"], w["ln2_b"])


MOE_TILE = 512
IDX_BLOCK = 1024
TOP_K = 2


def _router_kernel(x_ref, rt_ref, tri_ref, meta_ref, cnt_ref, run_ref):
    i = pl.program_id(0)

    @pl.when(i == 0)
    def _():
        run_ref[...] = jnp.zeros(run_ref.shape, F32)

    logits = jnp.dot(x_ref[...], rt_ref[...], preferred_element_type=F32, precision=lax.Precision.HIGHEST)
    lane = lax.broadcasted_iota(jnp.int32, logits.shape, 1).astype(F32)
    logits = jnp.where(lane < N_EXPERTS, logits, NEG)
    m1 = jnp.max(logits, axis=-1, keepdims=True)
    i1 = jnp.min(jnp.where(logits == m1, lane, float(LANE)), axis=-1, keepdims=True)
    rest = jnp.where(lane == i1, NEG, logits)
    m2 = jnp.max(rest, axis=-1, keepdims=True)
    i2 = jnp.min(jnp.where(rest == m2, lane, float(LANE)), axis=-1, keepdims=True)
    e2 = jnp.exp(m2 - m1)
    g1 = 1.0 / (1.0 + e2)
    g2 = e2 * g1
    onehot = jnp.where(lane == i1, 1.0, jnp.where(lane == i2, 1.0, 0.0))
    before = jnp.dot(tri_ref[...], onehot.astype(BF16), preferred_element_type=F32) + run_ref[...]
    r1 = jnp.sum(jnp.where(lane == i1, before, 0.0), axis=-1, keepdims=True)
    r2 = jnp.sum(jnp.where(lane == i2, before, 0.0), axis=-1, keepdims=True)
    run_ref[...] += jnp.sum(onehot, axis=0, keepdims=True)
    meta = jnp.zeros(logits.shape, F32)
    for col, val in enumerate((i1, i2, g1, g2, r1, r2)):
        meta = jnp.where(lane == col, val, meta)
    meta_ref[...] = meta
    cnt_ref[...] = run_ref[...]


def _gather_rows_kernel(src_ref, x_hbm, o_ref, buf, sem):
    def copy(r):
        tok = src_ref[r // LANE, r % LANE]
        return pltpu.make_async_copy(x_hbm.at[pl.ds(tok, 1)], buf.at[pl.ds(r, 1)], sem)

    def start(r, c):
        copy(r).start()
        return c

    def wait(r, c):
        copy(r).wait()
        return c

    lax.fori_loop(0, IDX_BLOCK, start, 0, unroll=8)
    lax.fori_loop(0, IDX_BLOCK, wait, 0, unroll=8)
    o_ref[...] = buf[...].astype(BF16)


def _experts_kernel(te_ref, nv_ref, x_ref, wg_ref, wu_ref, wd_ref, o_ref, acc_ref):
    i = pl.program_id(0)
    f = pl.program_id(1)
    last = pl.num_programs(1) - 1

    @pl.when(i < nv_ref[0])
    def _():
        @pl.when(f == 0)
        def _():
            acc_ref[...] = jnp.zeros(acc_ref.shape, F32)

        xb = x_ref[...]
        g = jnp.dot(xb, wg_ref[0], preferred_element_type=F32)
        u = jnp.dot(xb, wu_ref[0], preferred_element_type=F32)
        hmid = g * _sigmoid(g) * u
        acc_ref[...] += jnp.dot(hmid.astype(BF16), wd_ref[0], preferred_element_type=F32)

        @pl.when(f == last)
        def _():
            o_ref[...] = acc_ref[...]

    @pl.when((i >= nv_ref[0]) & (f == last))
    def _():
        o_ref[...] = jnp.zeros(o_ref.shape, F32)


def _combine_kernel(d1_ref, d2_ref, meta_ref, x_ref, g2_ref, b2_ref, ys_hbm, o_ref, buf1, buf2, sems):
    def copies(r):
        a = d1_ref[r // LANE, r % LANE]
        b = d2_ref[r // LANE, r % LANE]
        return (pltpu.make_async_copy(ys_hbm.at[pl.ds(a, 1)], buf1.at[pl.ds(r, 1)], sems.at[0]),
                pltpu.make_async_copy(ys_hbm.at[pl.ds(b, 1)], buf2.at[pl.ds(r, 1)], sems.at[1]))

    def start(r, c):
        for cp in copies(r):
            cp.start()
        return c

    def wait(r, c):
        for cp in copies(r):
            cp.wait()
        return c

    lax.fori_loop(0, IDX_BLOCK, start, 0, unroll=8)
    lax.fori_loop(0, IDX_BLOCK, wait, 0, unroll=8)
    meta = meta_ref[...]
    y = meta[:, 2:3] * buf1[...] + meta[:, 3:4] * buf2[...]
    o_ref[...] = _layer_norm(DEEPNORM_ALPHA * x_ref[...] + y, g2_ref[...], b2_ref[...])


def _moe(x, w):
    n = x.shape[0]
    ne, _, dff = w["moe_g"].shape
    assert n % IDX_BLOCK == 0 and ne == N_EXPERTS
    tr = IDX_BLOCK
    meta, cnt = pl.pallas_call(
        _router_kernel,
        grid=(n // tr,),
        in_specs=[pl.BlockSpec((tr, D_MODEL), lambda i: (i, 0)), _full((D_MODEL, LANE)), _full((tr, tr))],
        out_specs=[pl.BlockSpec((tr, LANE), lambda i: (i, 0)), _full((1, LANE))],
        out_shape=[jax.ShapeDtypeStruct((n, LANE), F32), jax.ShapeDtypeStruct((1, LANE), F32)],
        scratch_shapes=[pltpu.VMEM((1, LANE), F32)],
        compiler_params=_cparams(("arbitrary",)),
        name="moe_router",
    )(x, w["router"], w["tri"])

    i1 = meta[:, 0].astype(jnp.int32)
    i2 = meta[:, 1].astype(jnp.int32)
    counts = cnt[0, :ne].astype(jnp.int32)
    padded = (counts + MOE_TILE - 1) // MOE_TILE * MOE_TILE
    ends = jnp.cumsum(padded)
    starts = ends - padded
    d1 = starts[i1] + meta[:, 4].astype(jnp.int32)
    d2 = starts[i2] + meta[:, 5].astype(jnp.int32)
    a_pad = TOP_K * n + ne * MOE_TILE
    tok = jnp.arange(n, dtype=jnp.int32)
    src = jnp.zeros((a_pad,), jnp.int32).at[d1].set(tok).at[d2].set(tok)
    n_tiles = a_pad // MOE_TILE
    tile_expert = jnp.minimum(jnp.searchsorted(ends, jnp.arange(n_tiles, dtype=jnp.int32) * MOE_TILE,
                                               side="right"), ne - 1).astype(jnp.int32)
    n_valid = (ends[-1] // MOE_TILE).astype(jnp.int32).reshape(1)

    idx_spec = pl.BlockSpec((IDX_BLOCK // LANE, LANE), lambda i: (i, 0), memory_space=pltpu.SMEM)
    xs = pl.pallas_call(
        _gather_rows_kernel,
        grid=(a_pad // IDX_BLOCK,),
        in_specs=[idx_spec, pl.BlockSpec(memory_space=pl.ANY)],
        out_specs=pl.BlockSpec((IDX_BLOCK, D_MODEL), lambda i: (i, 0)),
        out_shape=jax.ShapeDtypeStruct((a_pad, D_MODEL), BF16),
        scratch_shapes=[pltpu.VMEM((IDX_BLOCK, D_MODEL), F32), pltpu.SemaphoreType.DMA(())],
        compiler_params=_cparams(("arbitrary",)),
        name="moe_gather",
    )(src.reshape(a_pad // LANE, LANE), x)

    tf = 896 if dff % 896 == 0 else 512
    ys = pl.pallas_call(
        _experts_kernel,
        grid_spec=pltpu.PrefetchScalarGridSpec(
            num_scalar_prefetch=2,
            grid=(n_tiles, dff // tf),
            in_specs=[pl.BlockSpec((MOE_TILE, D_MODEL), lambda i, f, te, nv: (i, 0)),
                      pl.BlockSpec((1, D_MODEL, tf), lambda i, f, te, nv: (te[i], 0, f)),
                      pl.BlockSpec((1, D_MODEL, tf), lambda i, f, te, nv: (te[i], 0, f)),
                      pl.BlockSpec((1, tf, D_MODEL), lambda i, f, te, nv: (te[i], f, 0))],
            out_specs=pl.BlockSpec((MOE_TILE, D_MODEL), lambda i, f, te, nv: (i, 0)),
            scratch_shapes=[pltpu.VMEM((MOE_TILE, D_MODEL), F32)],
        ),
        out_shape=jax.ShapeDtypeStruct((a_pad, D_MODEL), F32),
        compiler_params=_cparams(("arbitrary", "arbitrary")),
        name="moe_experts",
    )(tile_expert, n_valid, xs, w["moe_g"], w["moe_u"], w["moe_d"])

    vecd = _full((1, D_MODEL))
    row = lambda c: pl.BlockSpec((IDX_BLOCK, c), lambda i: (i, 0))
    return pl.pallas_call(
        _combine_kernel,
        grid=(n // IDX_BLOCK,),
        in_specs=[idx_spec, idx_spec, row(LANE), row(D_MODEL), vecd, vecd, pl.BlockSpec(memory_space=pl.ANY)],
        out_specs=row(D_MODEL),
        out_shape=jax.ShapeDtypeStruct((n, D_MODEL), F32),
        scratch_shapes=[pltpu.VMEM((IDX_BLOCK, D_MODEL), F32), pltpu.VMEM((IDX_BLOCK, D_MODEL), F32),
                        pltpu.SemaphoreType.DMA((2,))],
        compiler_params=_cparams(("arbitrary",)),
        name="moe_combine",
    )(d1.reshape(n // LANE, LANE), d2.reshape(n // LANE, LANE), meta, x, w["ln2_g"], w["ln2_b"], ys)


_RWKV_PERM = np.concatenate([np.arange(0, 256), np.arange(272, 528), np.arange(528, 784),
                             np.arange(256, 272), np.arange(784, 800), np.arange(800, 832)])
_RWKV_INV = np.argsort(_RWKV_PERM)


def _rot_cols(wr):
    half = MLA_ROPE // 2
    return jnp.concatenate([-wr[..., half:], wr[..., :half]], axis=-1)


def _pad_last(x, width):
    return jnp.pad(x, [(0, 0)] * (x.ndim - 1) + [(0, width - x.shape[-1])])


def _prep_layer(l, p):
    w = {}
    w_in = p["w_in"][l]
    rw = w_in[:, 256:256 + RWKV_COLS][:, _RWKV_PERM]
    mla = w_in[:, 256 + RWKV_COLS:]
    wkr = mla[:, MLA_Q_RANK + MLA_KV_RANK:]
    win = jnp.concatenate([w_in[:, :256], mla[:, :MLA_Q_RANK + MLA_KV_RANK], _pad_last(wkr, LANE),
                           _pad_last(_rot_cols(wkr), LANE), rw], axis=1)
    w["win"] = _pad_last(win, PIN_PAD).astype(BF16)
    wuq = p["mla_wuq"][l]
    wq_rope = wuq[:, :, MLA_NOPE:]
    w["wq"] = jnp.concatenate([_pad_last(wuq[:, :, :MLA_NOPE], LANE).reshape(MLA_Q_RANK, -1),
                               _pad_last(wq_rope, LANE).reshape(MLA_Q_RANK, -1),
                               _pad_last(_rot_cols(wq_rope), LANE).reshape(MLA_Q_RANK, -1)], axis=1).astype(BF16)
    wuk = jnp.transpose(p["mla_wuk"][l], (1, 2, 0))
    w["wuk"] = jnp.pad(wuk, ((0, 0), (0, LANE - MLA_NOPE), (0, 0))).astype(BF16)
    wuv = jnp.transpose(p["mla_wuv"][l], (1, 0, 2))
    eye = jnp.eye(MLA_HEADS, dtype=F32)
    w["wuv"] = (wuv[:, :, None, :] * eye[:, None, :, None]).reshape(MLA_HEADS, MLA_KV_RANK, -1).astype(BF16)
    w["qg"] = p["mla_qnorm_g"][l][None]
    w["kvg"] = p["mla_kvnorm_g"][l][None]
    pw = p["pool_w"][l]
    eye4 = jnp.eye(4, dtype=F32)
    w["pool_w"] = (pw[:, :, None, :] * eye4[:, None, :, None]).reshape(POOL_WIDTH, POOL_WIDTH).astype(BF16)
    w["pool_scale"] = p["pool_scale"][l][None]
    w["mu"] = p["rwkv_mu"][l][_RWKV_PERM][None]
    lora = jnp.zeros((64, 768), F32)
    lora = lora.at[0:16, 0:256].set(p["rwkv_w2"][l]).at[16:32, 256:512].set(p["rwkv_a2"][l])
    w["lora"] = lora.at[32:64, 512:768].set(p["rwkv_g2"][l])
    for name, key in (("w0", "rwkv_w0"), ("a0", "rwkv_a0"), ("kkp", "rwkv_kk"), ("ka", "rwkv_ka"),
                      ("lnx_g", "rwkv_lnx_g"), ("lnx_b", "rwkv_lnx_b")):
        w[name] = p[key][l][None]
    w["rk"] = p["rwkv_rk"][l].reshape(1, RWKV_WIDTH)
    w["bd"] = jnp.kron(jnp.eye(RWKV_HEADS, dtype=F32), jnp.ones((RWKV_HEAD, RWKV_HEAD), F32))
    if l > 0:
        w["v0"] = p["vres_v0"][l - 1][None]
        w["v1"] = _pad_last(p["vres_w1"][l - 1], LANE)
        w["v2"] = jnp.pad(p["vres_w2"][l - 1], ((0, LANE - p["vres_w2"].shape[1]), (0, 0)))
    w["wo"] = p["w_out"][l].astype(BF16)
    for name in ("ln1_g", "ln1_b", "ln2_g", "ln2_b"):
        w[name] = p[name][l][None]
    if l % 2 == 0:
        w["ffn_g"] = p["ffn_w_gate"][l // 2].astype(BF16)
        w["ffn_u"] = p["ffn_w_up"][l // 2].astype(BF16)
        w["ffn_d"] = p["ffn_w_down"][l // 2].astype(BF16)
    else:
        w["router"] = _pad_last(p["moe_router"][l // 2], LANE)
        tri = np.tril(np.ones((IDX_BLOCK, IDX_BLOCK), np.float32), -1)
        w["tri"] = jnp.asarray(tri, BF16)
        w["moe_g"] = p["moe_w_gate"][l // 2].astype(BF16)
        w["moe_u"] = p["moe_w_up"][l // 2].astype(BF16)
        w["moe_d"] = p["moe_w_down"][l // 2].astype(BF16)
    return w


def _rope_tables(pos):
    half = MLA_ROPE // 2
    freqs = ROPE_BASE ** (-jnp.arange(half, dtype=F32) / half)
    ang = pos.astype(F32)[:, None] * freqs[None, :]
    cos = jnp.cos(ang)
    sin = jnp.sin(ang)
    return (_pad_last(jnp.concatenate([cos, cos], axis=-1), LANE),
            _pad_last(jnp.concatenate([sin, sin], axis=-1), LANE))


def _state_to_kernel(s):
    return jnp.swapaxes(s, -1, -2).reshape(s.shape[0], RWKV_WIDTH, RWKV_HEAD)


def _state_from_kernel(s):
    return jnp.swapaxes(s.reshape(s.shape[0], RWKV_HEADS, RWKV_HEAD, RWKV_HEAD), -1, -2)


def _channel_mix(x_prompt, x_sample, l, w):
    if l % 2 == 0:
        return _ffn(x_prompt, w), _ffn(x_sample, w)
    n_prompt = x_prompt.shape[0]
    x = _moe(jnp.concatenate([x_prompt, x_sample], axis=0), w)
    return x[:n_prompt], x[n_prompt:]


def _run_prompt(x, layers):
    b, s, _ = x.shape
    xf = x.reshape(b * s, D_MODEL)
    cos, sin = _rope_tables(jnp.tile(jnp.arange(s, dtype=jnp.int32), b))
    vfirst = None
    outs = []
    for l, w in enumerate(layers):
        pa, pb, ckv, krope, kcat, qcat = _proj(xf, cos, sin, w)
        ya = _pool(pa, b, w, full_count=False)
        r, lw, k, v, kk, a, g = _rwkv_pre(pb, b, w, vfirst)
        if vfirst is None:
            vfirst = v
        y, sfin = _scan(r, lw, k, v, kk, a, jnp.zeros((b, RWKV_WIDTH, RWKV_HEAD), F32), c=_tile(s, 64))
        yc = _attn_prompt(qcat, kcat, w["wuv"], b)
        xf = yield _post(y, r, k, v, g, ya, yc, xf, w)
        outs.append((ckv.reshape(b, s, -1), krope.reshape(b, s, -1), pa.reshape(b, s, -1)[:, -POOL_BUF:],
                     pb.reshape(b, s, -1)[:, -1][:, _RWKV_INV], _state_from_kernel(sfin)))
    yield (xf.reshape(b, s, D_MODEL),) + tuple(jnp.stack([o[i] for o in outs]) for i in range(5))


def _run_sample(x, past_len, cache_ckv, cache_krope, state_pool, state_shift, state_wkv, page_table, layers):
    b, s, _ = x.shape
    xf = x.reshape(b * s, D_MODEL)
    cos, sin = _rope_tables(jnp.tile(past_len + jnp.arange(s, dtype=jnp.int32), b))
    pool_lead = -(POOL_BUF + s) % 8
    shift_lead = 7
    cache_krope_t = jnp.swapaxes(cache_krope, 2, 3)
    vfirst = None
    outs = []
    for l, w in enumerate(layers):
        pa, pb, ckv, krope, kcat, qcat = _proj(xf, cos, sin, w)
        ext = jnp.concatenate([state_pool[l], pa.reshape(b, s, -1)], axis=1)
        ext_p = jnp.pad(ext, ((0, 0), (pool_lead, 0), (0, 0)))
        ya = _pool(ext_p.reshape(-1, POOL_WIDTH), 1, w, full_count=True)
        ya = ya.reshape(b, -1, POOL_WIDTH)[:, -s:].reshape(b * s, POOL_WIDTH)
        pbe = jnp.concatenate([state_shift[l][:, None, _RWKV_PERM], pb.reshape(b, s, -1)], axis=1)
        pbe = jnp.pad(pbe, ((0, 0), (shift_lead, 0), (0, 0)))
        pre = _rwkv_pre(pbe.reshape(-1, RWKV_COLS), 1, w, vfirst)
        if vfirst is None:
            vfirst = pre[3]
        r, lw, k, v, kk, a, g = (t.reshape(b, -1, RWKV_WIDTH)[:, -s:].reshape(b * s, RWKV_WIDTH) for t in pre)
        y, sfin = _scan(r, lw, k, v, kk, a, _state_to_kernel(state_wkv[l]), c=s)
        yc = _attn_sample(qcat, kcat, w["wuv"], cache_ckv, cache_krope_t, page_table, l)
        xf = yield _post(y, r, k, v, g, ya, yc, xf, w)
        outs.append((ckv.reshape(b, s, -1), krope.reshape(b, s, -1), ext[:, -POOL_BUF:],
                     pb.reshape(b, s, -1)[:, -1][:, _RWKV_INV], _state_from_kernel(sfin)))
    yield (xf.reshape(b, s, D_MODEL),) + tuple(jnp.stack([o[i] for o in outs]) for i in range(5))


def kernel(x_prompt, x_sample, cache_ckv, cache_krope, state_pool, state_shift, state_wkv, page_table, ln1_g, ln1_b, ln2_g, ln2_b, w_in, pool_w, pool_scale, rwkv_mu, rwkv_w0, rwkv_w2, rwkv_a0, rwkv_a2, rwkv_g2, rwkv_kk, rwkv_ka, rwkv_rk, rwkv_lnx_g, rwkv_lnx_b, vres_v0, vres_w1, vres_w2, mla_qnorm_g, mla_wuq, mla_kvnorm_g, mla_wuk, mla_wuv, w_out, ffn_w_gate, ffn_w_up, ffn_w_down, moe_router, moe_w_gate, moe_w_up, moe_w_down):
    p = dict(ln1_g=ln1_g, ln1_b=ln1_b, ln2_g=ln2_g, ln2_b=ln2_b, w_in=w_in, pool_w=pool_w, pool_scale=pool_scale,
             rwkv_mu=rwkv_mu, rwkv_w0=rwkv_w0, rwkv_w2=rwkv_w2, rwkv_a0=rwkv_a0, rwkv_a2=rwkv_a2, rwkv_g2=rwkv_g2,
             rwkv_kk=rwkv_kk, rwkv_ka=rwkv_ka, rwkv_rk=rwkv_rk, rwkv_lnx_g=rwkv_lnx_g, rwkv_lnx_b=rwkv_lnx_b,
             vres_v0=vres_v0, vres_w1=vres_w1, vres_w2=vres_w2, mla_qnorm_g=mla_qnorm_g, mla_wuq=mla_wuq,
             mla_kvnorm_g=mla_kvnorm_g, mla_wuk=mla_wuk, mla_wuv=mla_wuv, w_out=w_out, ffn_w_gate=ffn_w_gate,
             ffn_w_up=ffn_w_up, ffn_w_down=ffn_w_down, moe_router=moe_router, moe_w_gate=moe_w_gate,
             moe_w_up=moe_w_up, moe_w_down=moe_w_down)
    layers = [_prep_layer(l, p) for l in range(DEPTH)]
    past_len = page_table.shape[1] * PAGE_SIZE
    prompt_trunk = _run_prompt(x_prompt, layers)
    sample_trunk = _run_sample(x_sample, past_len, cache_ckv, cache_krope, state_pool, state_shift, state_wkv,
                               page_table, layers)
    prompt, sample = next(prompt_trunk), next(sample_trunk)
    for l, w in enumerate(layers):
        x_p, x_s = _channel_mix(prompt, sample, l, w)
        prompt, sample = prompt_trunk.send(x_p), sample_trunk.send(x_s)
    return (prompt[0], sample[0]) + prompt[1:] + sample[1:]
```
